```python
import math
import jax, jax.numpy as jnp
from jax import lax
import numpy as np

D_MODEL = 2048
BATCH = 8
SEQ = 4096
DEPTH = 4
DEC_BATCH = 8
DEC_SEQ = 32
PAST_LEN = 4096

CHUNK = 64
N_EVEN = (DEPTH + 1) // 2
N_ODD = DEPTH // 2
D_A = D_MODEL // 2
CONV_A = 31
D_B = D_MODEL // 2
G_B = 8
DH_B = D_B // G_B
GMLP_CHUNK = 128
D_C = D_MODEL // 2
CONV_C = 3
N_HEADS = 16
Q_RANK = 512
KV_RANK = 512
NOPE_DIM = 128
ROPE_DIM = 64
QK_DIM = NOPE_DIM + ROPE_DIM
V_DIM = 128
ROPE_THETA = 10000.0
D_FF = 4 * D_MODEL
Q_BLOCK = 128
EPS = 1e-6
EVEN_IN = 2 * D_A + 2 * D_B
EVEN_OUT = D_A + D_B
ODD_IN = 3 * D_C + Q_RANK + KV_RANK + ROPE_DIM
ODD_OUT = D_C + N_HEADS * V_DIM

kernel_name = "hybrid_streaming_encoder_step"


def rms_norm(x, g):
    xf = x.astype(jnp.float32)
    y = xf * lax.rsqrt(jnp.mean(xf * xf, axis=-1, keepdims=True) + EPS)
    return y.astype(x.dtype) * g


def layer_norm(x, g, b):
    xf = x.astype(jnp.float32)
    xc = xf - jnp.mean(xf, axis=-1, keepdims=True)
    y = xc * lax.rsqrt(jnp.mean(xc * xc, axis=-1, keepdims=True) + EPS)
    return y.astype(x.dtype) * g + b


def rope(x, pos):
    half = x.shape[-1] // 2
    inv = ROPE_THETA ** (-jnp.arange(half, dtype=jnp.float32) / half)
    ang = pos.astype(jnp.float32)[:, None] * inv[None, :]
    cos = jnp.cos(ang)[None, :, None, :].astype(x.dtype)
    sin = jnp.sin(ang)[None, :, None, :].astype(x.dtype)
    x1, x2 = x[..., :half], x[..., half:]
    return jnp.concatenate([x1 * cos - x2 * sin, x2 * cos + x1 * sin], axis=-1)


def depthwise_causal_conv(buf, w):
    return lax.conv_general_dilated(
        buf, w[:, None, :], window_strides=(1,), padding="VALID",
        dimension_numbers=("NWC", "WIO", "NWC"), feature_group_count=w.shape[1])


def spatial_gate(u, v, w_s, b_s):
    bn, s, _ = v.shape
    L = min(s, GMLP_CHUNK)
    n = s // L
    vg = v.reshape(bn, n, L, G_B, DH_B)
    w = w_s[:, :L, :L] * jnp.tril(jnp.ones((L, L), w_s.dtype))[None]
    bias = jnp.transpose(b_s[:, :L])[None, None, :, :, None]
    mixed = jnp.einsum("gij,bnjgd->bnigd", w, vg) + bias
    return u * mixed.reshape(bn, s, D_B)


def chunk_causal_attention(q, k, v, q_pos, k_pos):
    bn, sq, h, dk = q.shape
    qb = min(Q_BLOCK, sq)
    nb = sq // qb
    scale = 1.0 / math.sqrt(dk)
    q_blocks = jnp.transpose(q.reshape(bn, nb, qb, h, dk), (1, 0, 2, 3, 4))
    p_blocks = q_pos.reshape(nb, qb)
    k_chunk = k_pos // CHUNK
    neg = jnp.finfo(jnp.float32).min

    def one_block(args):
        qi, pi = args
        s = jnp.einsum("bqhd,bkhd->bhqk", qi, k, preferred_element_type=jnp.float32) * scale
        mask = k_chunk[None, :] <= (pi // CHUNK)[:, None]
        p = jax.nn.softmax(jnp.where(mask[None, None], s, neg), axis=-1)
        return jnp.einsum("bhqk,bkhd->bqhd", p.astype(v.dtype), v)

    out = lax.map(one_block, (q_blocks, p_blocks))
    return jnp.transpose(out, (1, 0, 2, 3, 4)).reshape(bn, sq, h, v.shape[-1])


def even_mixer(h, conv_prev, w_in, conv_w, conv_b, ln_a_g, ln_a_b, ln_v_g, ln_v_b, w_s, b_s, w_out):
    z = h @ w_in
    a_val, a_gate, b_u, b_v = jnp.split(z, [D_A, 2 * D_A, 2 * D_A + D_B], axis=-1)
    a = a_val * jax.nn.sigmoid(a_gate)
    buf = jnp.concatenate([conv_prev, a], axis=1)
    conv = depthwise_causal_conv(buf, conv_w) + conv_b
    a_out = jax.nn.silu(layer_norm(conv, ln_a_g, ln_a_b))
    new_conv = buf[:, -(CONV_A - 1):]
    u = jax.nn.gelu(b_u, approximate=False)
    v = layer_norm(jax.nn.gelu(b_v, approximate=False), ln_v_g, ln_v_b)
    b_out = spatial_gate(u, v, w_s, b_s)
    out = jnp.concatenate([a_out, b_out], axis=-1) @ w_out
    return out, new_conv, v


def odd_mixer(h, conv_prev, lat_prev, kr_prev, w_in, conv_w, q_norm_g, w_uq, kv_norm_g, w_ukv, w_out):
    bn, s, _ = h.shape
    past = lat_prev.shape[1]
    z = h @ w_in
    g_b, g_c, x_c, z_q, z_kv, z_kr = jnp.split(
        z, [D_C, 2 * D_C, 3 * D_C, 3 * D_C + Q_RANK, 3 * D_C + Q_RANK + KV_RANK], axis=-1)
    buf = jnp.concatenate([conv_prev, g_c * x_c], axis=1)
    conv = buf[:, 0:s] * conv_w[0]
    for j in range(1, CONV_C):
        conv = conv + buf[:, j:j + s] * conv_w[j]
    c_out = g_b * conv
    new_conv = buf[:, -(CONV_C - 1):]

    q_pos = past + jnp.arange(s, dtype=jnp.int32)
    k_pos = jnp.arange(past + s, dtype=jnp.int32)
    q = (rms_norm(z_q, q_norm_g) @ w_uq).reshape(bn, s, N_HEADS, QK_DIM)
    q = jnp.concatenate([q[..., :NOPE_DIM], rope(q[..., NOPE_DIM:], q_pos)], axis=-1)
    c_kv = rms_norm(z_kv, kv_norm_g)
    k_r = rope(z_kr[:, :, None, :], q_pos)[:, :, 0, :]
    c_all = jnp.concatenate([lat_prev, c_kv], axis=1)
    kr_all = jnp.concatenate([kr_prev, k_r], axis=1)
    t = past + s
    kv = (c_all @ w_ukv).reshape(bn, t, N_HEADS, NOPE_DIM + V_DIM)
    k = jnp.concatenate(
        [kv[..., :NOPE_DIM], jnp.broadcast_to(kr_all[:, :, None, :], (bn, t, N_HEADS, ROPE_DIM))], axis=-1)
    v = kv[..., NOPE_DIM:]
    attn = chunk_causal_attention(q, k, v, q_pos, k_pos).reshape(bn, s, N_HEADS * V_DIM)
    out = jnp.concatenate([c_out, attn], axis=-1) @ w_out
    return out, new_conv, c_kv, k_r


def sq_relu_mlp(h, w_up, w_down):
    return jnp.square(jax.nn.relu(h @ w_up)) @ w_down


def setup_inputs(seed: int = 0) -> dict:
    key = jax.random.key(seed)
    ks = iter(jax.random.split(key, 40))

    def nrm(shape, scale):
        return jax.random.normal(next(ks), shape, jnp.float32) * scale

    def gain(shape):
        return 1.0 + nrm(shape, 0.01)

    return {
        "x_prompt": nrm((BATCH, SEQ, D_MODEL), 1.0),
        "x_sample": nrm((DEC_BATCH, DEC_SEQ, D_MODEL), 1.0),
        "state_conv_a": nrm((N_EVEN, DEC_BATCH, CONV_A - 1, D_A), 0.5),
        "state_conv_c": nrm((N_ODD, DEC_BATCH, CONV_C - 1, D_C), 1.0),
        "cache_kv_latent": nrm((N_ODD, DEC_BATCH, PAST_LEN, KV_RANK), 1.0),
        "cache_k_rope": nrm((N_ODD, DEC_BATCH, PAST_LEN, ROPE_DIM), 1.0),
        "norm_mix": gain((DEPTH, D_MODEL)),
        "norm_ffn": gain((DEPTH, D_MODEL)),
        "norm_final": gain((D_MODEL,)),
        "w_in_even": nrm((N_EVEN, D_MODEL, EVEN_IN), D_MODEL ** -0.5),
        "conv_a_w": nrm((N_EVEN, CONV_A, D_A), CONV_A ** -0.5),
        "conv_a_b": nrm((N_EVEN, D_A), 0.02),
        "ln_a_g": gain((N_EVEN, D_A)),
        "ln_a_b": nrm((N_EVEN, D_A), 0.02),
        "ln_v_g": gain((N_EVEN, D_B)),
        "ln_v_b": nrm((N_EVEN, D_B), 0.02),
        "w_spatial": nrm((N_EVEN, G_B, GMLP_CHUNK, GMLP_CHUNK), GMLP_CHUNK ** -0.5),
        "b_spatial": gain((N_EVEN, G_B, GMLP_CHUNK)),
        "w_out_even": nrm((N_EVEN, EVEN_OUT, D_MODEL), EVEN_OUT ** -0.5),
        "w_in_odd": nrm((N_ODD, D_MODEL, ODD_IN), D_MODEL ** -0.5),
        "conv_c_w": nrm((N_ODD, CONV_C, D_C), CONV_C ** -0.5),
        "q_norm_g": gain((N_ODD, Q_RANK)),
        "w_uq": nrm((N_ODD, Q_RANK, N_HEADS * QK_DIM), Q_RANK ** -0.5),
        "kv_norm_g": gain((N_ODD, KV_RANK)),
        "w_ukv": nrm((N_ODD, KV_RANK, N_HEADS * (NOPE_DIM + V_DIM)), KV_RANK ** -0.5),
        "w_out_odd": nrm((N_ODD, ODD_OUT, D_MODEL), ODD_OUT ** -0.5),
        "w_ffn_up": nrm((DEPTH, D_MODEL, D_FF), D_MODEL ** -0.5),
        "w_ffn_down": nrm((DEPTH, D_FF, D_MODEL), D_FF ** -0.5),
    }


def reference(x_prompt, x_sample, state_conv_a, state_conv_c, cache_kv_latent, cache_k_rope,
              norm_mix, norm_ffn, norm_final,
              w_in_even, conv_a_w, conv_a_b, ln_a_g, ln_a_b, ln_v_g, ln_v_b, w_spatial, b_spatial, w_out_even,
              w_in_odd, conv_c_w, q_norm_g, w_uq, kv_norm_g, w_ukv, w_out_odd,
              w_ffn_up, w_ffn_down):
    hp, hs = x_prompt, x_sample
    bp = hp.shape[0]
    ca_p, ca_s, gv_s, cc_p, cc_s, lat_p, kr_p, lat_s, kr_s = [], [], [], [], [], [], [], [], []
    for l in range(DEPTH):
        p = l // 2
        np_ = rms_norm(hp, norm_mix[l])
        ns_ = rms_norm(hs, norm_mix[l])
        if l % 2 == 0:
            ew = (w_in_even[p], conv_a_w[p], conv_a_b[p], ln_a_g[p], ln_a_b[p],
                  ln_v_g[p], ln_v_b[p], w_spatial[p], b_spatial[p], w_out_even[p])
            zeros_a = jnp.zeros((bp, CONV_A - 1, D_A), hp.dtype)
            op, buf_p, _ = even_mixer(np_, zeros_a, *ew)
            os_, buf_s, v_s = even_mixer(ns_, state_conv_a[p], *ew)
            ca_p.append(buf_p)
            ca_s.append(buf_s)
            gv_s.append(v_s)
        else:
            ow = (w_in_odd[p], conv_c_w[p], q_norm_g[p], w_uq[p], kv_norm_g[p], w_ukv[p], w_out_odd[p])
            zeros_c = jnp.zeros((bp, CONV_C - 1, D_C), hp.dtype)
            no_lat = jnp.zeros((bp, 0, KV_RANK), hp.dtype)
            no_kr = jnp.zeros((bp, 0, ROPE_DIM), hp.dtype)
            op, bufc_p, c_p, r_p = odd_mixer(np_, zeros_c, no_lat, no_kr, *ow)
            os_, bufc_s, c_s, r_s = odd_mixer(ns_, state_conv_c[p], cache_kv_latent[p], cache_k_rope[p], *ow)
            cc_p.append(bufc_p)
            cc_s.append(bufc_s)
            lat_p.append(c_p)
            kr_p.append(r_p)
            lat_s.append(c_s)
            kr_s.append(r_s)
        hp = hp + op
        hs = hs + os_
        hp = hp + sq_relu_mlp(rms_norm(hp, norm_ffn[l]), w_ffn_up[l], w_ffn_down[l])
        hs = hs + sq_relu_mlp(rms_norm(hs, norm_ffn[l]), w_ffn_up[l], w_ffn_down[l])
    y_prompt = rms_norm(hp, norm_final)
    y_sample = rms_norm(hs, norm_final)
    conv_a_prompt = jnp.stack(ca_p)
    conv_a_sample = jnp.stack(ca_s)
    gmlp_v_sample = jnp.stack(gv_s)
    conv_c_prompt = jnp.stack(cc_p)
    conv_c_sample = jnp.stack(cc_s)
    kv_latent_prompt = jnp.stack(lat_p)
    k_rope_prompt = jnp.stack(kr_p)
    kv_latent_sample = jnp.stack(lat_s)
    k_rope_sample = jnp.stack(kr_s)
    return (y_prompt, y_sample, conv_a_prompt, conv_a_sample, gmlp_v_sample, conv_c_prompt, conv_c_sample,
            kv_latent_prompt, k_rope_prompt, kv_latent_sample, k_rope_sample)
```

```python
import functools
import math

import jax
import jax.numpy as jnp
from jax import lax
from jax.experimental import pallas as pl
from jax.experimental.pallas import tpu as pltpu

F32 = jnp.float32
BF16 = jnp.bfloat16

CHUNK = 64
CHUNK_SHIFT = 6
CONV_A = 31
CONV_C = 3
G_B = 8
GMLP_CHUNK = 128
N_HEADS = 16
NOPE_DIM = 128
ROPE_DIM = 64
V_DIM = 128
ROPE_THETA = 10000.0
EPS = 1e-6

LANES = 128
HEAD_PAD = 2 * LANES
HALO_A = 32
HALO_C = 8
MASK_VALUE = -1e30
VMEM_LIMIT = 56 * 1024 * 1024

TM_MATMUL = 1024
TN_MATMUL = 1024
TF_FFN = 512
TS_EVEN = 128
TS_ODD = 512
TQ_ATT = 512
TK_ATT = 512


def _params(*sem):
    return pltpu.CompilerParams(dimension_semantics=sem, vmem_limit_bytes=VMEM_LIMIT)


def _rms(x, g):
    return x * lax.rsqrt(jnp.mean(x * x, axis=-1, keepdims=True) + EPS) * g


def _layer_norm(x, g, b):
    xc = x - jnp.mean(x, axis=-1, keepdims=True)
    return xc * lax.rsqrt(jnp.mean(xc * xc, axis=-1, keepdims=True) + EPS) * g + b


def _gelu(x):
    return 0.5 * x * (1.0 + lax.erf(x * math.sqrt(0.5)))


def _tile(n, t):
    t = min(n, t)
    assert n % t == 0, (n, t)
    return t


def _norm_matmul_body(x_ref, g_ref, w_ref, o_ref, xn_ref):
    @pl.when(pl.program_id(1) == 0)
    def _():
        xn_ref[...] = _rms(x_ref[...], g_ref[...]).astype(BF16)

    o_ref[...] = jnp.dot(xn_ref[...], w_ref[...], preferred_element_type=F32)


def _norm_matmul(x, g, w, tn):
    t, d = x.shape
    n = w.shape[1]
    tm = _tile(t, TM_MATMUL)
    return pl.pallas_call(
        _norm_matmul_body,
        grid=(t // tm, n // tn),
        in_specs=[pl.BlockSpec((tm, d), lambda i, j: (i, 0)),
                  pl.BlockSpec((1, d), lambda i, j: (0, 0)),
                  pl.BlockSpec((d, tn), lambda i, j: (0, j))],
        out_specs=pl.BlockSpec((tm, tn), lambda i, j: (i, j)),
        out_shape=jax.ShapeDtypeStruct((t, n), F32),
        scratch_shapes=[pltpu.VMEM((tm, d), BF16)],
        compiler_params=_params("parallel", "arbitrary"),
        name="norm_matmul",
    )(x, g.reshape(1, d), w)


def _even_mid_body(*refs, ts, da, gl, want_v):
    (zc_ref, zh_ref, hist_ref, cw_ref, cb_ref, lag_ref, lab_ref, lvg_ref, lvb_ref,
     wsp_ref, bsp_ref, cat_ref, tail_ref) = refs[:13]
    v_ref = refs[13] if want_v else None
    buf_ref, conv_ref = refs[-2:]
    i = pl.program_id(1)

    buf_ref[HALO_A:HALO_A + ts, :] = zc_ref[0, :, 0:da] * jax.nn.sigmoid(zc_ref[0, :, da:2 * da])

    @pl.when(i == 0)
    def _():
        buf_ref[0:HALO_A, :] = hist_ref[0]

    @pl.when(i > 0)
    def _():
        buf_ref[0:HALO_A, :] = zh_ref[0, :, 0:da] * jax.nn.sigmoid(zh_ref[0, :, da:2 * da])

    tail_ref[0] = buf_ref[ts:ts + HALO_A, :]

    rc = min(ts, 64)
    base = HALO_A - (CONV_A - 1)
    for c in range(da // LANES):
        cs = slice(c * LANES, (c + 1) * LANES)
        for r0 in range(0, ts, rc):
            acc = jnp.broadcast_to(cb_ref[:, cs], (rc, LANES))
            for j in range(CONV_A):
                acc = acc + cw_ref[j:j + 1, cs] * buf_ref[r0 + base + j:r0 + base + j + rc, cs]
            conv_ref[r0:r0 + rc, cs] = acc

    y = _layer_norm(conv_ref[...], lag_ref[...], lab_ref[...])
    cat_ref[0, :, 0:da] = (y * jax.nn.sigmoid(y)).astype(BF16)

    u = _gelu(zc_ref[0, :, 2 * da:3 * da])
    v = _layer_norm(_gelu(zc_ref[0, :, 3 * da:4 * da]), lvg_ref[...], lvb_ref[...])
    if want_v:
        v_ref[0] = v
    vb = v.astype(BF16)
    dh = da // G_B
    tril = lax.broadcasted_iota(jnp.int32, (gl, gl), 0) >= lax.broadcasted_iota(jnp.int32, (gl, gl), 1)
    for g in range(G_B):
        wg = jnp.where(tril, wsp_ref[g], 0.0).astype(BF16)
        gs = slice(g * dh, (g + 1) * dh)
        for k in range(ts // gl):
            rs = slice(k * gl, (k + 1) * gl)
            mixed = jnp.dot(wg, vb[rs, gs], preferred_element_type=F32) + bsp_ref[:, g:g + 1]
            cat_ref[0, rs, da + g * dh:da + (g + 1) * dh] = (u[rs, gs] * mixed).astype(BF16)


def _even_mid(z3, hist, conv_w, conv_b, lag, lab, lvg, lvb, w_sp, b_sp_t, want_v):
    b, s, n4 = z3.shape
    da = n4 // 4
    gl = w_sp.shape[-1]
    ts = _tile(s, TS_EVEN)
    assert ts % gl == 0 and ts % HALO_A == 0
    hb = ts // HALO_A
    row = lambda bi, i: (0, 0)
    out_shape = [jax.ShapeDtypeStruct((b, s, 2 * da), BF16), jax.ShapeDtypeStruct((b, HALO_A, da), F32)]
    out_specs = [pl.BlockSpec((1, ts, 2 * da), lambda bi, i: (bi, i, 0)),
                 pl.BlockSpec((1, HALO_A, da), lambda bi, i: (bi, 0, 0))]
    if want_v:
        out_shape.append(jax.ShapeDtypeStruct((b, s, da), F32))
        out_specs.append(pl.BlockSpec((1, ts, da), lambda bi, i: (bi, i, 0)))
    return pl.pallas_call(
        functools.partial(_even_mid_body, ts=ts, da=da, gl=gl, want_v=want_v),
        grid=(b, s // ts),
        in_specs=[pl.BlockSpec((1, ts, n4), lambda bi, i: (bi, i, 0)),
                  pl.BlockSpec((1, HALO_A, 2 * da), lambda bi, i: (bi, jnp.maximum(i * hb - 1, 0), 0)),
                  pl.BlockSpec((1, HALO_A, da), lambda bi, i: (bi, 0, 0)),
                  pl.BlockSpec((HALO_A, da), row),
                  pl.BlockSpec((1, da), row), pl.BlockSpec((1, da), row), pl.BlockSpec((1, da), row),
                  pl.BlockSpec((1, da), row), pl.BlockSpec((1, da), row),
                  pl.BlockSpec((G_B, gl, gl), lambda bi, i: (0, 0, 0)),
                  pl.BlockSpec((gl, G_B), row)],
        out_specs=out_specs,
        out_shape=out_shape,
        scratch_shapes=[pltpu.VMEM((HALO_A + ts, da), F32), pltpu.VMEM((ts, da), F32)],
        compiler_params=_params("parallel", "arbitrary"),
        name="even_mid",
    )(z3, z3, hist, conv_w, conv_b, lag, lab, lvg, lvb, w_sp, b_sp_t)


def _proj_residual_body(*refs, n_in):
    x_refs, w_refs = refs[:n_in], refs[n_in:2 * n_in]
    r_ref, o_ref = refs[2 * n_in], refs[2 * n_in + 1]
    acc = r_ref[...]
    for x_ref, w_ref in zip(x_refs, w_refs):
        acc = acc + jnp.dot(x_ref[...], w_ref[...], preferred_element_type=F32)
    o_ref[...] = acc


def _proj_residual(xs, ws, res):
    t, n = res.shape
    tm = _tile(t, TM_MATMUL)
    tn = _tile(n, TN_MATMUL)
    in_specs = [pl.BlockSpec((tm, x.shape[1]), lambda i, j: (i, 0)) for x in xs]
    in_specs += [pl.BlockSpec((w.shape[0], tn), lambda i, j: (0, j)) for w in ws]
    in_specs.append(pl.BlockSpec((tm, tn), lambda i, j: (i, j)))
    return pl.pallas_call(
        functools.partial(_proj_residual_body, n_in=len(xs)),
        grid=(t // tm, n // tn),
        in_specs=in_specs,
        out_specs=pl.BlockSpec((tm, tn), lambda i, j: (i, j)),
        out_shape=jax.ShapeDtypeStruct((t, n), F32),
        compiler_params=_params("parallel", "arbitrary"),
        name="proj_residual",
    )(*xs, *ws, res)


def _ffn_body(x_ref, g_ref, wu_ref, wd_ref, gf_ref, o_ref, xn_ref, *, final_norm):
    j = pl.program_id(1)

    @pl.when(j == 0)
    def _():
        x = x_ref[...]
        xn_ref[...] = _rms(x, g_ref[...]).astype(BF16)
        o_ref[...] = x

    a = jnp.maximum(jnp.dot(xn_ref[...], wu_ref[...], preferred_element_type=F32), 0.0)
    o_ref[...] += jnp.dot((a * a).astype(BF16), wd_ref[...], preferred_element_type=F32)

    if final_norm:
        @pl.when(j == pl.num_programs(1) - 1)
        def _():
            o_ref[...] = _rms(o_ref[...], gf_ref[...])


def _ffn(x, g, wu, wd, gf, final_norm):
    t, d = x.shape
    f = wu.shape[1]
    tm = _tile(t, TM_MATMUL)
    tf = _tile(f, TF_FFN)
    return pl.pallas_call(
        functools.partial(_ffn_body, final_norm=final_norm),
        grid=(t // tm, f // tf),
        in_specs=[pl.BlockSpec((tm, d), lambda i, j: (i, 0)),
                  pl.BlockSpec((1, d), lambda i, j: (0, 0)),
                  pl.BlockSpec((d, tf), lambda i, j: (0, j)),
                  pl.BlockSpec((tf, d), lambda i, j: (j, 0)),
                  pl.BlockSpec((1, d), lambda i, j: (0, 0))],
        out_specs=pl.BlockSpec((tm, d), lambda i, j: (i, 0)),
        out_shape=jax.ShapeDtypeStruct((t, d), F32),
        scratch_shapes=[pltpu.VMEM((tm, d), BF16)],
        compiler_params=_params("parallel", "arbitrary"),
        name="ffn",
    )(x, g.reshape(1, d), wu, wd, gf.reshape(1, d))


def _rope_fold(y, cs):
    t = y * cs
    return t + pltpu.roll(t, ROPE_DIM, axis=1)


def _odd_mid_body(zc_ref, hc_ref, hx_ref, hist_ref, cw_ref, qg_ref, kvg_ref, cs_ref,
                  cout_ref, qn_ref, ckv_ref, kr_ref, tail_ref, buf_ref, *, ts, dc, rq, rkv):
    i = pl.program_id(1)
    gx = zc_ref[0, :, dc:2 * dc] * zc_ref[0, :, 2 * dc:3 * dc]
    buf_ref[HALO_C:HALO_C + ts, :] = gx

    @pl.when(i == 0)
    def _():
        buf_ref[0:HALO_C, :] = hist_ref[0]

    @pl.when(i > 0)
    def _():
        buf_ref[0:HALO_C, :] = hc_ref[0] * hx_ref[0]

    tail_ref[0] = buf_ref[ts:ts + HALO_C, :]
    conv = cw_ref[CONV_C - 1:CONV_C, :] * gx
    for j in range(CONV_C - 1):
        off = HALO_C - (CONV_C - 1) + j
        conv = conv + cw_ref[j:j + 1, :] * buf_ref[off:off + ts, :]
    cout_ref[0] = (zc_ref[0, :, 0:dc] * conv).astype(BF16)

    o = 3 * dc
    qn_ref[0] = _rms(zc_ref[0, :, o:o + rq], qg_ref[...]).astype(BF16)
    ckv_ref[0] = _rms(zc_ref[0, :, o + rq:o + rq + rkv], kvg_ref[...])
    kr = _rope_fold(zc_ref[0, :, o + rq + rkv:o + rq + rkv + LANES], cs_ref[...])
    lane = lax.broadcasted_iota(jnp.int32, kr.shape, 1)
    kr_ref[0] = jnp.where(lane < ROPE_DIM, kr, 0.0)


def _odd_mid(z3, hist, conv_w, qg, kvg, cs_tab, dc, rq, rkv):
    b, s, n = z3.shape
    ts = _tile(s, TS_ODD)
    assert ts % HALO_C == 0 and n == 3 * dc + rq + rkv + LANES and dc % 1024 == 0
    hb = ts // HALO_C
    halo = lambda col: pl.BlockSpec((1, HALO_C, dc), lambda bi, i: (bi, jnp.maximum(i * hb - 1, 0), col))
    row = lambda bi, i: (0, 0)
    tok = lambda w: pl.BlockSpec((1, ts, w), lambda bi, i: (bi, i, 0))
    return pl.pallas_call(
        functools.partial(_odd_mid_body, ts=ts, dc=dc, rq=rq, rkv=rkv),
        grid=(b, s // ts),
        in_specs=[tok(n), halo(1), halo(2),
                  pl.BlockSpec((1, HALO_C, dc), lambda bi, i: (bi, 0, 0)),
                  pl.BlockSpec((CONV_C, dc), row),
                  pl.BlockSpec((1, rq), row), pl.BlockSpec((1, rkv), row),
                  pl.BlockSpec((ts, LANES), lambda bi, i: (i, 0))],
        out_specs=[tok(dc), tok(rq), tok(rkv), tok(LANES),
                   pl.BlockSpec((1, HALO_C, dc), lambda bi, i: (bi, 0, 0))],
        out_shape=[jax.ShapeDtypeStruct((b, s, dc), BF16),
                   jax.ShapeDtypeStruct((b, s, rq), BF16),
                   jax.ShapeDtypeStruct((b, s, rkv), F32),
                   jax.ShapeDtypeStruct((b, s, LANES), F32),
                   jax.ShapeDtypeStruct((b, HALO_C, dc), F32)],
        scratch_shapes=[pltpu.VMEM((HALO_C + ts, dc), F32)],
        compiler_params=_params("parallel", "arbitrary"),
        name="odd_mid",
    )(z3, z3, z3, hist, conv_w, qg, kvg, cs_tab)


def _q_up_body(qn_ref, w_ref, cs_ref, o_ref, *, scale):
    acc = jnp.dot(qn_ref[...], w_ref[...], preferred_element_type=F32)
    cs = cs_ref[...]
    for h in range(acc.shape[1] // HEAD_PAD):
        lo = h * HEAD_PAD
        o_ref[:, lo:lo + LANES] = (acc[:, lo:lo + LANES] * scale).astype(BF16)
        o_ref[:, lo + LANES:lo + HEAD_PAD] = (_rope_fold(acc[:, lo + LANES:lo + HEAD_PAD], cs) * scale).astype(BF16)


def _q_up(qn, w, cs_tab, scale):
    t, r = qn.shape
    n = w.shape[1]
    tm = _tile(t, TM_MATMUL)
    tn = _tile(n, TN_MATMUL)
    period = cs_tab.shape[0] // tm
    return pl.pallas_call(
        functools.partial(_q_up_body, scale=scale),
        grid=(t // tm, n // tn),
        in_specs=[pl.BlockSpec((tm, r), lambda i, j: (i, 0)),
                  pl.BlockSpec((r, tn), lambda i, j: (0, j)),
                  pl.BlockSpec((tm, LANES), lambda i, j: (i % period, 0))],
        out_specs=pl.BlockSpec((tm, tn), lambda i, j: (i, j)),
        out_shape=jax.ShapeDtypeStruct((t, n), BF16),
        compiler_params=_params("parallel", "arbitrary"),
        name="q_up",
    )(qn, w, cs_tab)


def _kv_up_body(c_ref, kr_ref, wk_ref, wv_ref, k_ref, v_ref):
    c = c_ref[...].astype(BF16)
    k = jnp.dot(c, wk_ref[...], preferred_element_type=F32)
    v_ref[...] = jnp.dot(c, wv_ref[...], preferred_element_type=F32).astype(BF16)
    kr = kr_ref[...].astype(BF16)
    for h in range(k.shape[1] // NOPE_DIM):
        k_ref[:, h * HEAD_PAD:h * HEAD_PAD + NOPE_DIM] = k[:, h * NOPE_DIM:(h + 1) * NOPE_DIM].astype(BF16)
        k_ref[:, h * HEAD_PAD + NOPE_DIM:(h + 1) * HEAD_PAD] = kr


def _kv_up(c, kr, wk, wv):
    t, r = c.shape
    tm = _tile(t, TM_MATMUL)
    hpt = 4
    return pl.pallas_call(
        _kv_up_body,
        grid=(t // tm, N_HEADS // hpt),
        in_specs=[pl.BlockSpec((tm, r), lambda i, j: (i, 0)),
                  pl.BlockSpec((tm, LANES), lambda i, j: (i, 0)),
                  pl.BlockSpec((r, hpt * NOPE_DIM), lambda i, j: (0, j)),
                  pl.BlockSpec((r, hpt * V_DIM), lambda i, j: (0, j))],
        out_specs=[pl.BlockSpec((tm, hpt * HEAD_PAD), lambda i, j: (i, j)),
                   pl.BlockSpec((tm, hpt * V_DIM), lambda i, j: (i, j))],
        out_shape=[jax.ShapeDtypeStruct((t, N_HEADS * HEAD_PAD), BF16),
                   jax.ShapeDtypeStruct((t, N_HEADS * V_DIM), BF16)],
        compiler_params=_params("parallel", "arbitrary"),
        name="kv_up",
    )(c, kr, wk, wv)


def _attention_body(q_ref, k_ref, v_ref, o_ref, *, tq, tk, past, t_valid):
    q = q_ref[0]
    q0 = past + pl.program_id(2) * tq
    see_all = jnp.minimum(((q0 // CHUNK) + 1) * CHUNK, t_valid)
    see_any = jnp.minimum((((q0 + tq - 1) // CHUNK) + 1) * CHUNK, t_valid)
    n_full = see_all // tk
    n_all = (see_any + tk - 1) // tk

    def step(kt, carry, masked):
        m, l, acc = carry
        rows = pl.ds(pl.multiple_of(kt * tk, tk), tk)
        s = lax.dot_general(q, k_ref[0, rows, :], (((1,), (1,)), ((), ())), preferred_element_type=F32)
        if masked:
            q_pos = q0 + lax.broadcasted_iota(jnp.int32, (tq, 1), 0)
            k_pos = kt * tk + lax.broadcasted_iota(jnp.int32, (1, tk), 1)
            ok = jnp.logical_and(k_pos >> CHUNK_SHIFT <= q_pos >> CHUNK_SHIFT, k_pos < t_valid)
            s = jnp.where(ok, s, MASK_VALUE)
        m_new = jnp.maximum(m, jnp.max(s, axis=-1, keepdims=True))
        alpha = jnp.exp(m - m_new)
        p = jnp.exp(s - m_new)
        l = alpha * l + jnp.sum(p, axis=-1, keepdims=True)
        acc = alpha * acc + jnp.dot(p.astype(BF16), v_ref[0, rows, :], preferred_element_type=F32)
        return m_new, l, acc

    init = (jnp.full((tq, 1), MASK_VALUE, F32), jnp.zeros((tq, 1), F32), jnp.zeros((tq, V_DIM), F32))
    carry = lax.fori_loop(0, n_full, functools.partial(step, masked=False), init)
    _, l, acc = lax.fori_loop(n_full, n_all, functools.partial(step, masked=True), carry)
    o_ref[0] = (acc / l).astype(BF16)


def _attention(q3, k3, v3, past, t_valid):
    b, sq, _ = q3.shape
    skv = k3.shape[1]
    tq = _tile(sq, TQ_ATT)
    tk = _tile(skv, TK_ATT)
    return pl.pallas_call(
        functools.partial(_attention_body, tq=tq, tk=tk, past=past, t_valid=t_valid),
        grid=(b, N_HEADS, sq // tq),
        in_specs=[pl.BlockSpec((1, tq, HEAD_PAD), lambda bi, h, i: (bi, i, h)),
                  pl.BlockSpec((1, skv, HEAD_PAD), lambda bi, h, i: (bi, 0, h)),
                  pl.BlockSpec((1, skv, V_DIM), lambda bi, h, i: (bi, 0, h))],
        out_specs=pl.BlockSpec((1, tq, V_DIM), lambda bi, h, i: (bi, i, h)),
        out_shape=jax.ShapeDtypeStruct((b, sq, N_HEADS * V_DIM), BF16),
        compiler_params=_params("parallel", "parallel", "arbitrary"),
        name="attention",
    )(q3, k3, v3)


def _rope_table(past, s):
    half = ROPE_DIM // 2
    inv = ROPE_THETA ** (-jnp.arange(half, dtype=F32) / half)
    ang = (past + jnp.arange(s, dtype=jnp.int32)).astype(F32)[:, None] * inv[None, :]
    cos, sin = jnp.cos(ang), jnp.sin(ang)
    return jnp.concatenate([cos, cos, sin, sin], axis=-1)


def _rot_cols(w):
    half = ROPE_DIM // 2
    return jnp.concatenate([-w[..., half:], w[..., :half]], axis=-1)


def _pad_rows_front(x, rows):
    return jnp.pad(x, ((0, 0), (rows - x.shape[1], 0), (0, 0)))


def _even_layer(h, b, s, hist, wts, want_v):
    (g_mix, w_in, conv_w, conv_b, lag, lab, lvg, lvb, w_sp, b_sp_t, w_out) = wts
    z = _norm_matmul(h, g_mix, w_in, _tile(w_in.shape[1], TN_MATMUL))
    outs = _even_mid(z.reshape(b, s, -1), hist, conv_w, conv_b, lag, lab, lvg, lvb, w_sp, b_sp_t, want_v)
    cat, tail = outs[0], outs[1]
    h = _proj_residual([cat.reshape(b * s, -1)], [w_out], h)
    return h, tail[:, HALO_A - (CONV_A - 1):], (outs[2] if want_v else None)


def _odd_layer(h, b, s, hist, lat_prev, kr_prev, wts, cs_tab, cs_tab_q):
    (g_mix, w_in, conv_w, qg, w_q, kvg, w_k, w_v, w_out_c, w_out_a) = wts
    dc = conv_w.shape[1]
    rq, rkv = qg.shape[1], kvg.shape[1]
    z = _norm_matmul(h, g_mix, w_in, w_in.shape[1] // 3)
    cout, qn, ckv, kr, tail = _odd_mid(z.reshape(b, s, -1), hist, conv_w, qg, kvg, cs_tab, dc, rq, rkv)
    scale = 1.0 / math.sqrt(NOPE_DIM + ROPE_DIM)
    q = _q_up(qn.reshape(b * s, rq), w_q, cs_tab_q, scale)
    past = 0 if lat_prev is None else lat_prev.shape[1]
    t_valid = past + s
    if lat_prev is None:
        c_all, kr_all = ckv, kr
    else:
        kr_prev = jnp.pad(kr_prev, ((0, 0), (0, 0), (0, LANES - ROPE_DIM)))
        c_all = jnp.concatenate([lat_prev, ckv], axis=1)
        kr_all = jnp.concatenate([kr_prev, kr], axis=1)
    skv = -(-t_valid // TK_ATT) * TK_ATT if t_valid > TK_ATT else t_valid
    if skv != t_valid:
        c_all = jnp.pad(c_all, ((0, 0), (0, skv - t_valid), (0, 0)))
        kr_all = jnp.pad(kr_all, ((0, 0), (0, skv - t_valid), (0, 0)))
    k, v = _kv_up(c_all.reshape(b * skv, rkv), kr_all.reshape(b * skv, LANES), w_k, w_v)
    attn = _attention(q.reshape(b, s, -1), k.reshape(b, skv, -1), v.reshape(b, skv, -1), past, t_valid)
    h = _proj_residual([cout.reshape(b * s, dc), attn.reshape(b * s, -1)], [w_out_c, w_out_a], h)
    return h, tail[:, HALO_C - (CONV_C - 1):], ckv, kr[..., :ROPE_DIM]


def kernel(x_prompt, x_sample, state_conv_a, state_conv_c, cache_kv_latent, cache_k_rope, norm_mix, norm_ffn, norm_final, w_in_even, conv_a_w, conv_a_b, ln_a_g, ln_a_b, ln_v_g, ln_v_b, w_spatial, b_spatial, w_out_even, w_in_odd, conv_c_w, q_norm_g, w_uq, kv_norm_g, w_ukv, w_out_odd, w_ffn_up, w_ffn_down):
    bp, sp, d = x_prompt.shape
    bs, ss, _ = x_sample.shape
    depth = norm_mix.shape[0]
    da = conv_a_w.shape[-1]
    dc = conv_c_w.shape[-1]
    rq, rkv = q_norm_g.shape[-1], kv_norm_g.shape[-1]
    past = cache_kv_latent.shape[2]
    assert sp % GMLP_CHUNK == 0 and ss <= GMLP_CHUNK and ss >= HALO_A and past % GMLP_CHUNK == 0

    hp = x_prompt.reshape(bp * sp, d)
    hs = x_sample.reshape(bs * ss, d)
    cs_p = _rope_table(0, sp)
    cs_s = _rope_table(past, ss)
    tm_p, tm_s = _tile(bp * sp, TM_MATMUL), _tile(bs * ss, TM_MATMUL)
    cs_p_q = cs_p if tm_p <= sp else jnp.tile(cs_p, (tm_p // sp, 1))
    cs_s_q = cs_s if tm_s <= ss else jnp.tile(cs_s, (tm_s // ss, 1))

    ca_p, ca_s, gv_s, cc_p, cc_s, lat_p, kr_p, lat_s, kr_s = [], [], [], [], [], [], [], [], []
    for l in range(depth):
        p = l // 2
        if l % 2 == 0:
            row = lambda a: a[p].reshape(1, -1)
            gl_p, gl_s = min(sp, GMLP_CHUNK), min(ss, GMLP_CHUNK)
            common = (norm_mix[l], w_in_even[p].astype(BF16),
                      jnp.pad(conv_a_w[p], ((0, HALO_A - CONV_A), (0, 0))), row(conv_a_b),
                      row(ln_a_g), row(ln_a_b), row(ln_v_g), row(ln_v_b))
            w_out = w_out_even[p].astype(BF16)
            wts_p = common + (w_spatial[p][:, :gl_p, :gl_p], b_spatial[p][:, :gl_p].T, w_out)
            wts_s = common + (w_spatial[p][:, :gl_s, :gl_s], b_spatial[p][:, :gl_s].T, w_out)
            hp, tail_p, _ = _even_layer(hp, bp, sp, jnp.zeros((bp, HALO_A, da), F32), wts_p, False)
            hs, tail_s, v_s = _even_layer(hs, bs, ss, _pad_rows_front(state_conv_a[p], HALO_A), wts_s, True)
            ca_p.append(tail_p)
            ca_s.append(tail_s)
            gv_s.append(v_s)
        else:
            w_in = w_in_odd[p]
            n_main = 3 * dc + rq + rkv
            w_in = jnp.concatenate([w_in, _rot_cols(w_in[:, n_main:])], axis=1).astype(BF16)
            wq = w_uq[p].reshape(rq, N_HEADS, NOPE_DIM + ROPE_DIM)
            wq = jnp.concatenate([wq, _rot_cols(wq[..., NOPE_DIM:])], axis=-1).reshape(rq, N_HEADS * HEAD_PAD).astype(BF16)
            wkv = w_ukv[p].reshape(rkv, N_HEADS, NOPE_DIM + V_DIM)
            wk = wkv[..., :NOPE_DIM].reshape(rkv, N_HEADS * NOPE_DIM).astype(BF16)
            wv = wkv[..., NOPE_DIM:].reshape(rkv, N_HEADS * V_DIM).astype(BF16)
            w_out = w_out_odd[p].astype(BF16)
            wts = (norm_mix[l], w_in, conv_c_w[p], q_norm_g[p].reshape(1, rq), wq,
                   kv_norm_g[p].reshape(1, rkv), wk, wv, w_out[:dc], w_out[dc:])
            hp, tail_p, c_p, r_p = _odd_layer(hp, bp, sp, jnp.zeros((bp, HALO_C, dc), F32), None, None,
                                              wts, cs_p, cs_p_q)
            hs, tail_s, c_s, r_s = _odd_layer(hs, bs, ss, _pad_rows_front(state_conv_c[p], HALO_C),
                                              cache_kv_latent[p], cache_k_rope[p], wts, cs_s, cs_s_q)
            cc_p.append(tail_p)
            cc_s.append(tail_s)
            lat_p.append(c_p)
            kr_p.append(r_p)
            lat_s.append(c_s)
            kr_s.append(r_s)
        wu, wd = w_ffn_up[l].astype(BF16), w_ffn_down[l].astype(BF16)
        last = l == depth - 1
        hp = _ffn(hp, norm_ffn[l], wu, wd, norm_final, last)
        hs = _ffn(hs, norm_ffn[l], wu, wd, norm_final, last)

    return (hp.reshape(bp, sp, d), hs.reshape(bs, ss, d),
            jnp.stack(ca_p), jnp.stack(ca_s), jnp.stack(gv_s), jnp.stack(cc_p), jnp.stack(cc_s),
            jnp.stack(lat_p), jnp.stack(kr_p), jnp.stack(lat_s), jnp.stack(kr_s))
```

```python
import functools
import math

import jax
import jax.numpy as jnp
from jax import lax
from jax.experimental import pallas as pl
from jax.experimental.pallas import tpu as pltpu

F32 = jnp.float32
BF16 = jnp.bfloat16

CHUNK = 64
CHUNK_SHIFT = 6
CONV_A = 31
CONV_C = 3
G_B = 8
GMLP_CHUNK = 128
N_HEADS = 16
NOPE_DIM = 128
ROPE_DIM = 64
V_DIM = 128
V_ROWS = V_DIM + 16
ROPE_THETA = 10000.0
EPS = 1e-6

LANES = 128
HEAD_PAD = 2 * LANES
HALO_A = 32
HALO_C = 8
MASK_VALUE = -1e30
VMEM_LIMIT = 56 * 1024 * 1024

TM_MATMUL = 1024
TN_MATMUL = 1024
TF_FFN = 512
TS_EVEN = 128
TS_ODD = 512
TQ_ATT = 512
TK_ATT = 256
KV_ROWS_STEP = 1024


def _params(*sem):
    return pltpu.CompilerParams(dimension_semantics=sem, vmem_limit_bytes=VMEM_LIMIT)


def _rms(x, g):
    return x * lax.rsqrt(jnp.mean(x * x, axis=-1, keepdims=True) + EPS) * g


def _layer_norm(x, g, b):
    xc = x - jnp.mean(x, axis=-1, keepdims=True)
    return xc * lax.rsqrt(jnp.mean(xc * xc, axis=-1, keepdims=True) + EPS) * g + b


def _gelu(x):
    return 0.5 * x * (1.0 + lax.erf(x * math.sqrt(0.5)))


def _tile(n, t):
    t = min(n, t)
    assert n % t == 0, (n, t)
    return t


def _norm_matmul_body(x_ref, g_ref, w_ref, o_ref, xn_ref):
    @pl.when(pl.program_id(1) == 0)
    def _():
        xn_ref[...] = _rms(x_ref[...], g_ref[...]).astype(BF16)

    o_ref[...] = jnp.dot(xn_ref[...], w_ref[...], preferred_element_type=F32)


def _norm_matmul(x, g, w, tn):
    t, d = x.shape
    n = w.shape[1]
    tm = _tile(t, TM_MATMUL)
    return pl.pallas_call(
        _norm_matmul_body,
        grid=(t // tm, n // tn),
        in_specs=[pl.BlockSpec((tm, d), lambda i, j: (i, 0)),
                  pl.BlockSpec((1, d), lambda i, j: (0, 0)),
                  pl.BlockSpec((d, tn), lambda i, j: (0, j))],
        out_specs=pl.BlockSpec((tm, tn), lambda i, j: (i, j)),
        out_shape=jax.ShapeDtypeStruct((t, n), F32),
        scratch_shapes=[pltpu.VMEM((tm, d), BF16)],
        compiler_params=_params("parallel", "arbitrary"),
        name="norm_matmul",
    )(x, g.reshape(1, d), w)


def _even_mid_body(*refs, ts, da, gl, want_v):
    (zc_ref, zh_ref, hist_ref, cw_ref, cb_ref, lag_ref, lab_ref, lvg_ref, lvb_ref,
     wsp_ref, bsp_ref, cat_ref, tail_ref) = refs[:13]
    v_ref = refs[13] if want_v else None
    buf_ref, conv_ref = refs[-2:]
    i = pl.program_id(1)

    buf_ref[HALO_A:HALO_A + ts, :] = zc_ref[0, :, 0:da] * jax.nn.sigmoid(zc_ref[0, :, da:2 * da])

    @pl.when(i == 0)
    def _():
        buf_ref[0:HALO_A, :] = hist_ref[0]

    @pl.when(i > 0)
    def _():
        buf_ref[0:HALO_A, :] = zh_ref[0, :, 0:da] * jax.nn.sigmoid(zh_ref[0, :, da:2 * da])

    tail_ref[0] = buf_ref[ts:ts + HALO_A, :]

    rc = min(ts, 64)
    base = HALO_A - (CONV_A - 1)
    for c in range(da // LANES):
        cs = slice(c * LANES, (c + 1) * LANES)
        for r0 in range(0, ts, rc):
            acc = jnp.broadcast_to(cb_ref[:, cs], (rc, LANES))
            for j in range(CONV_A):
                acc = acc + cw_ref[j:j + 1, cs] * buf_ref[r0 + base + j:r0 + base + j + rc, cs]
            conv_ref[r0:r0 + rc, cs] = acc

    y = _layer_norm(conv_ref[...], lag_ref[...], lab_ref[...])
    cat_ref[0, :, 0:da] = (y * jax.nn.sigmoid(y)).astype(BF16)

    u = _gelu(zc_ref[0, :, 2 * da:3 * da])
    v = _layer_norm(_gelu(zc_ref[0, :, 3 * da:4 * da]), lvg_ref[...], lvb_ref[...])
    if want_v:
        v_ref[0] = v
    vb = v.astype(BF16)
    dh = da // G_B
    tril = lax.broadcasted_iota(jnp.int32, (gl, gl), 0) >= lax.broadcasted_iota(jnp.int32, (gl, gl), 1)
    for g in range(G_B):
        wg = jnp.where(tril, wsp_ref[g], 0.0).astype(BF16)
        gs = slice(g * dh, (g + 1) * dh)
        for k in range(ts // gl):
            rs = slice(k * gl, (k + 1) * gl)
            mixed = jnp.dot(wg, vb[rs, gs], preferred_element_type=F32) + bsp_ref[:, g:g + 1]
            cat_ref[0, rs, da + g * dh:da + (g + 1) * dh] = (u[rs, gs] * mixed).astype(BF16)


def _even_mid(z3, hist, conv_w, conv_b, lag, lab, lvg, lvb, w_sp, b_sp_t, want_v):
    b, s, n4 = z3.shape
    da = n4 // 4
    gl = w_sp.shape[-1]
    ts = _tile(s, TS_EVEN)
    assert ts % gl == 0 and ts % HALO_A == 0
    hb = ts // HALO_A
    row = lambda bi, i: (0, 0)
    out_shape = [jax.ShapeDtypeStruct((b, s, 2 * da), BF16), jax.ShapeDtypeStruct((b, HALO_A, da), F32)]
    out_specs = [pl.BlockSpec((1, ts, 2 * da), lambda bi, i: (bi, i, 0)),
                 pl.BlockSpec((1, HALO_A, da), lambda bi, i: (bi, 0, 0))]
    if want_v:
        out_shape.append(jax.ShapeDtypeStruct((b, s, da), F32))
        out_specs.append(pl.BlockSpec((1, ts, da), lambda bi, i: (bi, i, 0)))
    return pl.pallas_call(
        functools.partial(_even_mid_body, ts=ts, da=da, gl=gl, want_v=want_v),
        grid=(b, s // ts),
        in_specs=[pl.BlockSpec((1, ts, n4), lambda bi, i: (bi, i, 0)),
                  pl.BlockSpec((1, HALO_A, 2 * da), lambda bi, i: (bi, jnp.maximum(i * hb - 1, 0), 0)),
                  pl.BlockSpec((1, HALO_A, da), lambda bi, i: (bi, 0, 0)),
                  pl.BlockSpec((HALO_A, da), row),
                  pl.BlockSpec((1, da), row), pl.BlockSpec((1, da), row), pl.BlockSpec((1, da), row),
                  pl.BlockSpec((1, da), row), pl.BlockSpec((1, da), row),
                  pl.BlockSpec((G_B, gl, gl), lambda bi, i: (0, 0, 0)),
                  pl.BlockSpec((gl, G_B), row)],
        out_specs=out_specs,
        out_shape=out_shape,
        scratch_shapes=[pltpu.VMEM((HALO_A + ts, da), F32), pltpu.VMEM((ts, da), F32)],
        compiler_params=_params("parallel", "arbitrary"),
        name="even_mid",
    )(z3, z3, hist, conv_w, conv_b, lag, lab, lvg, lvb, w_sp, b_sp_t)


def _proj_residual_body(*refs, n_in):
    x_refs, w_refs = refs[:n_in], refs[n_in:2 * n_in]
    r_ref, o_ref = refs[2 * n_in], refs[2 * n_in + 1]
    acc = r_ref[...]
    for x_ref, w_ref in zip(x_refs, w_refs):
        acc = acc + jnp.dot(x_ref[...], w_ref[...], preferred_element_type=F32)
    o_ref[...] = acc


def _proj_residual(xs, ws, res):
    t, n = res.shape
    tm = _tile(t, TM_MATMUL)
    tn = _tile(n, TN_MATMUL)
    in_specs = [pl.BlockSpec((tm, x.shape[1]), lambda i, j: (i, 0)) for x in xs]
    in_specs += [pl.BlockSpec((w.shape[0], tn), lambda i, j: (0, j)) for w in ws]
    in_specs.append(pl.BlockSpec((tm, tn), lambda i, j: (i, j)))
    return pl.pallas_call(
        functools.partial(_proj_residual_body, n_in=len(xs)),
        grid=(t // tm, n // tn),
        in_specs=in_specs,
        out_specs=pl.BlockSpec((tm, tn), lambda i, j: (i, j)),
        out_shape=jax.ShapeDtypeStruct((t, n), F32),
        compiler_params=_params("parallel", "arbitrary"),
        name="proj_residual",
    )(*xs, *ws, res)


def _ffn_body(x_ref, g_ref, wu_ref, wd_ref, gf_ref, o_ref, xn_ref, *, final_norm):
    j = pl.program_id(1)

    @pl.when(j == 0)
    def _():
        x = x_ref[...]
        xn_ref[...] = _rms(x, g_ref[...]).astype(BF16)
        o_ref[...] = x

    a = jnp.maximum(jnp.dot(xn_ref[...], wu_ref[...], preferred_element_type=F32), 0.0)
    o_ref[...] += jnp.dot((a * a).astype(BF16), wd_ref[...], preferred_element_type=F32)

    if final_norm:
        @pl.when(j == pl.num_programs(1) - 1)
        def _():
            o_ref[...] = _rms(o_ref[...], gf_ref[...])


def _ffn(x, g, wu, wd, gf, final_norm):
    t, d = x.shape
    f = wu.shape[1]
    tm = _tile(t, TM_MATMUL)
    tf = _tile(f, TF_FFN)
    return pl.pallas_call(
        functools.partial(_ffn_body, final_norm=final_norm),
        grid=(t // tm, f // tf),
        in_specs=[pl.BlockSpec((tm, d), lambda i, j: (i, 0)),
                  pl.BlockSpec((1, d), lambda i, j: (0, 0)),
                  pl.BlockSpec((d, tf), lambda i, j: (0, j)),
                  pl.BlockSpec((tf, d), lambda i, j: (j, 0)),
                  pl.BlockSpec((1, d), lambda i, j: (0, 0))],
        out_specs=pl.BlockSpec((tm, d), lambda i, j: (i, 0)),
        out_shape=jax.ShapeDtypeStruct((t, d), F32),
        scratch_shapes=[pltpu.VMEM((tm, d), BF16)],
        compiler_params=_params("parallel", "arbitrary"),
        name="ffn",
    )(x, g.reshape(1, d), wu, wd, gf.reshape(1, d))


def _rope_fold(y, cs):
    t = y * cs
    return t + pltpu.roll(t, ROPE_DIM, axis=1)


def _odd_mid_body(zc_ref, hc_ref, hx_ref, hist_ref, cw_ref, qg_ref, kvg_ref, cs_ref,
                  cout_ref, qn_ref, ckv_ref, kr_ref, tail_ref, buf_ref, *, ts, dc, rq, rkv):
    i = pl.program_id(1)
    gx = zc_ref[0, :, dc:2 * dc] * zc_ref[0, :, 2 * dc:3 * dc]
    buf_ref[HALO_C:HALO_C + ts, :] = gx

    @pl.when(i == 0)
    def _():
        buf_ref[0:HALO_C, :] = hist_ref[0]

    @pl.when(i > 0)
    def _():
        buf_ref[0:HALO_C, :] = hc_ref[0] * hx_ref[0]

    tail_ref[0] = buf_ref[ts:ts + HALO_C, :]
    conv = cw_ref[CONV_C - 1:CONV_C, :] * gx
    for j in range(CONV_C - 1):
        off = HALO_C - (CONV_C - 1) + j
        conv = conv + cw_ref[j:j + 1, :] * buf_ref[off:off + ts, :]
    cout_ref[0] = (zc_ref[0, :, 0:dc] * conv).astype(BF16)

    o = 3 * dc
    qn_ref[0] = _rms(zc_ref[0, :, o:o + rq], qg_ref[...]).astype(BF16)
    ckv_ref[0] = _rms(zc_ref[0, :, o + rq:o + rq + rkv], kvg_ref[...])
    kr = _rope_fold(zc_ref[0, :, o + rq + rkv:o + rq + rkv + LANES], cs_ref[...])
    lane = lax.broadcasted_iota(jnp.int32, kr.shape, 1)
    kr_ref[0] = jnp.where(lane < ROPE_DIM, kr, 0.0)


def _odd_mid(z3, hist, conv_w, qg, kvg, cs_tab, dc, rq, rkv):
    b, s, n = z3.shape
    ts = _tile(s, TS_ODD)
    assert ts % HALO_C == 0 and n == 3 * dc + rq + rkv + LANES and dc % 1024 == 0
    hb = ts // HALO_C
    halo = lambda col: pl.BlockSpec((1, HALO_C, dc), lambda bi, i: (bi, jnp.maximum(i * hb - 1, 0), col))
    row = lambda bi, i: (0, 0)
    tok = lambda w: pl.BlockSpec((1, ts, w), lambda bi, i: (bi, i, 0))
    return pl.pallas_call(
        functools.partial(_odd_mid_body, ts=ts, dc=dc, rq=rq, rkv=rkv),
        grid=(b, s // ts),
        in_specs=[tok(n), halo(1), halo(2),
                  pl.BlockSpec((1, HALO_C, dc), lambda bi, i: (bi, 0, 0)),
                  pl.BlockSpec((CONV_C, dc), row),
                  pl.BlockSpec((1, rq), row), pl.BlockSpec((1, rkv), row),
                  pl.BlockSpec((ts, LANES), lambda bi, i: (i, 0))],
        out_specs=[tok(dc), tok(rq), tok(rkv), tok(LANES),
                   pl.BlockSpec((1, HALO_C, dc), lambda bi, i: (bi, 0, 0))],
        out_shape=[jax.ShapeDtypeStruct((b, s, dc), BF16),
                   jax.ShapeDtypeStruct((b, s, rq), BF16),
                   jax.ShapeDtypeStruct((b, s, rkv), F32),
                   jax.ShapeDtypeStruct((b, s, LANES), F32),
                   jax.ShapeDtypeStruct((b, HALO_C, dc), F32)],
        scratch_shapes=[pltpu.VMEM((HALO_C + ts, dc), F32)],
        compiler_params=_params("parallel", "arbitrary"),
        name="odd_mid",
    )(z3, z3, z3, hist, conv_w, qg, kvg, cs_tab)


def _q_up_body(w_ref, qn_ref, cs_ref, o_ref, *, scale):
    acc = lax.dot_general(w_ref[...], qn_ref[...], (((1,), (1,)), ((), ())), preferred_element_type=F32)
    cs = cs_ref[...]
    for h in range(acc.shape[0] // HEAD_PAD):
        lo = h * HEAD_PAD
        o_ref[0, lo:lo + NOPE_DIM, :] = (acc[lo:lo + NOPE_DIM] * scale).astype(BF16)
        t = acc[lo + NOPE_DIM:lo + HEAD_PAD] * cs
        o_ref[0, lo + NOPE_DIM:lo + NOPE_DIM + ROPE_DIM, :] = ((t[:ROPE_DIM] + t[ROPE_DIM:]) * scale).astype(BF16)
        o_ref[0, lo + NOPE_DIM + ROPE_DIM:lo + HEAD_PAD, :] = jnp.zeros((ROPE_DIM, acc.shape[1]), BF16)


def _q_up(qn, w_t, cs_t, b, s, scale):
    r = qn.shape[1]
    n = w_t.shape[0]
    tm = _tile(s, TM_MATMUL)
    tn = _tile(n, TN_MATMUL)
    per = s // tm
    return pl.pallas_call(
        functools.partial(_q_up_body, scale=scale),
        grid=(b * per, n // tn),
        in_specs=[pl.BlockSpec((tn, r), lambda i, j: (j, 0)),
                  pl.BlockSpec((tm, r), lambda i, j: (i, 0)),
                  pl.BlockSpec((LANES, tm), lambda i, j: (0, i % per))],
        out_specs=pl.BlockSpec((1, tn, tm), lambda i, j: (i // per, j, i % per)),
        out_shape=jax.ShapeDtypeStruct((b, n, s), BF16),
        compiler_params=_params("parallel", "arbitrary"),
        name="q_up",
    )(w_t, qn, cs_t)


def _kv_up_body(c_ref, kr_ref, wk_ref, wvt_ref, k_ref, vt_ref):
    c = c_ref[...].astype(BF16)
    k = jnp.dot(c, wk_ref[...], preferred_element_type=F32)
    vt = lax.dot_general(wvt_ref[...], c, (((1,), (1,)), ((), ())), preferred_element_type=F32)
    kr = kr_ref[...].astype(BF16)
    tk = vt_ref.shape[-1]
    for h in range(k.shape[1] // NOPE_DIM):
        k_ref[:, h * HEAD_PAD:h * HEAD_PAD + NOPE_DIM] = k[:, h * NOPE_DIM:(h + 1) * NOPE_DIM].astype(BF16)
        k_ref[:, h * HEAD_PAD + NOPE_DIM:(h + 1) * HEAD_PAD] = kr
        for t in range(vt_ref.shape[2]):
            vt_ref[0, h, t, 0:V_DIM, :] = vt[h * V_DIM:(h + 1) * V_DIM, t * tk:(t + 1) * tk].astype(BF16)
            vt_ref[0, h, t, V_DIM:V_ROWS, :] = jnp.ones((V_ROWS - V_DIM, tk), BF16)


def _kv_up(c, kr, wk, wv_t, b, skv, tk):
    t, r = c.shape
    nkt = skv // tk
    tm = _tile(skv, KV_ROWS_STEP)
    kts = tm // tk
    per = skv // tm
    hpt = 4
    return pl.pallas_call(
        _kv_up_body,
        grid=(t // tm, N_HEADS // hpt),
        in_specs=[pl.BlockSpec((tm, r), lambda i, j: (i, 0)),
                  pl.BlockSpec((tm, LANES), lambda i, j: (i, 0)),
                  pl.BlockSpec((r, hpt * NOPE_DIM), lambda i, j: (0, j)),
                  pl.BlockSpec((hpt * V_DIM, r), lambda i, j: (j, 0))],
        out_specs=[pl.BlockSpec((tm, hpt * HEAD_PAD), lambda i, j: (i, j)),
                   pl.BlockSpec((1, hpt, kts, V_ROWS, tk), lambda i, j: (i // per, j, i % per, 0, 0))],
        out_shape=[jax.ShapeDtypeStruct((t, N_HEADS * HEAD_PAD), BF16),
                   jax.ShapeDtypeStruct((b, N_HEADS, nkt, V_ROWS, tk), BF16)],
        compiler_params=_params("parallel", "arbitrary"),
        name="kv_up",
    )(c, kr, wk, wv_t)


def _colmax(s):
    while s.shape[0] > 8 and s.shape[0] % 16 == 0:
        half = s.shape[0] // 2
        s = jnp.maximum(s[:half], s[half:])
    return jnp.max(s, axis=0, keepdims=True)


def _attention_body(qt_ref, k_ref, vt_ref, o_ref, s_ref, p_ref, *, tq, tk, nkt, past, t_valid):
    qt = qt_ref[0]
    q0 = past + pl.program_id(2) * tq
    see_all = jnp.minimum(((q0 >> CHUNK_SHIFT) + 1) * CHUNK, t_valid)
    see_any = jnp.minimum((((q0 + tq - 1) >> CHUNK_SHIFT) + 1) * CHUNK, t_valid)
    pairs_full = (see_all // tk) // 2
    pairs_all = ((see_any + tk - 1) // tk + 1) // 2

    def k_tile(kt):
        rows = pl.ds(pl.multiple_of(jnp.minimum(kt, nkt - 1) * tk, tk), tk)
        return k_ref[0, rows, :]

    def half_step(kt, carry, slot, masked):
        m, alpha, acc = carry
        other = 1 - slot
        pv = jnp.dot(vt_ref[0, 0, jnp.clip(kt - 1, 0, nkt - 1)], p_ref[other], preferred_element_type=F32)
        s_next = jnp.dot(k_tile(kt + 1), qt, preferred_element_type=F32)
        s = s_ref[slot]
        if masked:
            k_pos = kt * tk + lax.broadcasted_iota(jnp.int32, (tk, 1), 0)
            q_pos = q0 + lax.broadcasted_iota(jnp.int32, (1, tq), 1)
            k_chunk = jnp.where(k_pos < t_valid, k_pos >> CHUNK_SHIFT, jnp.iinfo(jnp.int32).max)
            s = jnp.where(k_chunk <= q_pos >> CHUNK_SHIFT, s, MASK_VALUE)
        m_new = jnp.maximum(m, _colmax(s))
        p_ref[slot] = jnp.exp2(s - m_new).astype(BF16)
        s_ref[other] = s_next
        return m_new, jnp.exp2(m - m_new), alpha * acc + pv

    def pair(i, carry, masked):
        carry = half_step(2 * i, carry, 0, masked)
        return half_step(2 * i + 1, carry, 1, masked)

    s_ref[0] = jnp.dot(k_tile(0), qt, preferred_element_type=F32)
    p_ref[1] = jnp.zeros((tk, tq), BF16)
    init = (jnp.full((1, tq), MASK_VALUE, F32), jnp.ones((1, tq), F32), jnp.zeros((V_ROWS, tq), F32))
    carry = lax.fori_loop(0, pairs_full, functools.partial(pair, masked=False), init)
    _, alpha, acc = lax.fori_loop(pairs_full, pairs_all, functools.partial(pair, masked=True), carry)
    last = jnp.minimum(2 * pairs_all - 1, nkt - 1)
    acc = alpha * acc + jnp.dot(vt_ref[0, 0, last], p_ref[1], preferred_element_type=F32)
    o_ref[0] = jnp.transpose(acc[:V_DIM] / acc[V_DIM:V_DIM + 1]).astype(BF16)


def _attention(qt, k3, vt, past, t_valid):
    b, _, sq = qt.shape
    skv = k3.shape[1]
    nkt, tk = vt.shape[2], vt.shape[4]
    tq = _tile(sq, TQ_ATT)
    return pl.pallas_call(
        functools.partial(_attention_body, tq=tq, tk=tk, nkt=nkt, past=past, t_valid=t_valid),
        scratch_shapes=[pltpu.VMEM((2, tk, tq), F32), pltpu.VMEM((2, tk, tq), BF16)],
        grid=(b, N_HEADS, sq // tq),
        in_specs=[pl.BlockSpec((1, HEAD_PAD, tq), lambda bi, h, i: (bi, h, i)),
                  pl.BlockSpec((1, skv, HEAD_PAD), lambda bi, h, i: (bi, 0, h)),
                  pl.BlockSpec((1, 1, nkt, V_ROWS, tk), lambda bi, h, i: (bi, h, 0, 0, 0))],
        out_specs=pl.BlockSpec((1, tq, V_DIM), lambda bi, h, i: (bi, i, h)),
        out_shape=jax.ShapeDtypeStruct((b, sq, N_HEADS * V_DIM), BF16),
        compiler_params=_params("parallel", "parallel", "arbitrary"),
        name="attention",
    )(qt, k3, vt)


def _rope_table(past, s):
    half = ROPE_DIM // 2
    inv = ROPE_THETA ** (-jnp.arange(half, dtype=F32) / half)
    ang = (past + jnp.arange(s, dtype=jnp.int32)).astype(F32)[:, None] * inv[None, :]
    cos, sin = jnp.cos(ang), jnp.sin(ang)
    return jnp.concatenate([cos, cos, sin, sin], axis=-1)


def _rot_cols(w):
    half = ROPE_DIM // 2
    return jnp.concatenate([-w[..., half:], w[..., :half]], axis=-1)


def _pad_rows_front(x, rows):
    return jnp.pad(x, ((0, 0), (rows - x.shape[1], 0), (0, 0)))


def _even_layer(h, b, s, hist, wts, want_v):
    (g_mix, w_in, conv_w, conv_b, lag, lab, lvg, lvb, w_sp, b_sp_t, w_out) = wts
    z = _norm_matmul(h, g_mix, w_in, _tile(w_in.shape[1], TN_MATMUL))
    outs = _even_mid(z.reshape(b, s, -1), hist, conv_w, conv_b, lag, lab, lvg, lvb, w_sp, b_sp_t, want_v)
    cat, tail = outs[0], outs[1]
    h = _proj_residual([cat.reshape(b * s, -1)], [w_out], h)
    return h, tail[:, HALO_A - (CONV_A - 1):], (outs[2] if want_v else None)


def _odd_layer(h, b, s, hist, lat_prev, kr_prev, wts, cs_tab, cs_q_t):
    (g_mix, w_in, conv_w, qg, wq_t, kvg, w_k, wv_t, w_out_c, w_out_a) = wts
    dc = conv_w.shape[1]
    rq, rkv = qg.shape[1], kvg.shape[1]
    z = _norm_matmul(h, g_mix, w_in, w_in.shape[1] // 3)
    cout, qn, ckv, kr, tail = _odd_mid(z.reshape(b, s, -1), hist, conv_w, qg, kvg, cs_tab, dc, rq, rkv)
    sq = cs_q_t.shape[1]
    if sq != s:
        qn = jnp.pad(qn, ((0, 0), (0, sq - s), (0, 0)))
    scale = math.log2(math.e) / math.sqrt(NOPE_DIM + ROPE_DIM)
    qt = _q_up(qn.reshape(b * sq, rq), wq_t, cs_q_t, b, sq, scale)
    past = 0 if lat_prev is None else lat_prev.shape[1]
    t_valid = past + s
    if lat_prev is None:
        c_all, kr_all = ckv, kr
    else:
        kr_prev = jnp.pad(kr_prev, ((0, 0), (0, 0), (0, LANES - ROPE_DIM)))
        c_all = jnp.concatenate([lat_prev, ckv], axis=1)
        kr_all = jnp.concatenate([kr_prev, kr], axis=1)
    tk = min(TK_ATT, t_valid)
    step = KV_ROWS_STEP if t_valid > KV_ROWS_STEP else tk
    skv = -(-t_valid // step) * step
    if skv != t_valid:
        c_all = jnp.pad(c_all, ((0, 0), (0, skv - t_valid), (0, 0)))
        kr_all = jnp.pad(kr_all, ((0, 0), (0, skv - t_valid), (0, 0)))
    k, vt = _kv_up(c_all.reshape(b * skv, rkv), kr_all.reshape(b * skv, LANES), w_k, wv_t, b, skv, tk)
    attn = _attention(qt, k.reshape(b, skv, -1), vt, past, t_valid)[:, :s]
    h = _proj_residual([cout.reshape(b * s, dc), attn.reshape(b * s, -1)], [w_out_c, w_out_a], h)
    return h, tail[:, HALO_C - (CONV_C - 1):], ckv, kr[..., :ROPE_DIM]


def kernel(x_prompt, x_sample, state_conv_a, state_conv_c, cache_kv_latent, cache_k_rope, norm_mix, norm_ffn, norm_final, w_in_even, conv_a_w, conv_a_b, ln_a_g, ln_a_b, ln_v_g, ln_v_b, w_spatial, b_spatial, w_out_even, w_in_odd, conv_c_w, q_norm_g, w_uq, kv_norm_g, w_ukv, w_out_odd, w_ffn_up, w_ffn_down):
    bp, sp, d = x_prompt.shape
    bs, ss, _ = x_sample.shape
    depth = norm_mix.shape[0]
    da = conv_a_w.shape[-1]
    dc = conv_c_w.shape[-1]
    rq, rkv = q_norm_g.shape[-1], kv_norm_g.shape[-1]
    past = cache_kv_latent.shape[2]
    assert sp % GMLP_CHUNK == 0 and ss <= GMLP_CHUNK and ss >= HALO_A and past % GMLP_CHUNK == 0

    hp = x_prompt.reshape(bp * sp, d)
    hs = x_sample.reshape(bs * ss, d)
    cs_p = _rope_table(0, sp)
    cs_s = _rope_table(past, ss)
    cs_p_q = cs_p.T
    cs_s_q = _rope_table(past, -(-ss // LANES) * LANES).T

    ca_p, ca_s, gv_s, cc_p, cc_s, lat_p, kr_p, lat_s, kr_s = [], [], [], [], [], [], [], [], []
    for l in range(depth):
        p = l // 2
        if l % 2 == 0:
            row = lambda a: a[p].reshape(1, -1)
            gl_p, gl_s = min(sp, GMLP_CHUNK), min(ss, GMLP_CHUNK)
            common = (norm_mix[l], w_in_even[p].astype(BF16),
                      jnp.pad(conv_a_w[p], ((0, HALO_A - CONV_A), (0, 0))), row(conv_a_b),
                      row(ln_a_g), row(ln_a_b), row(ln_v_g), row(ln_v_b))
            w_out = w_out_even[p].astype(BF16)
            wts_p = common + (w_spatial[p][:, :gl_p, :gl_p], b_spatial[p][:, :gl_p].T, w_out)
            wts_s = common + (w_spatial[p][:, :gl_s, :gl_s], b_spatial[p][:, :gl_s].T, w_out)
            hp, tail_p, _ = _even_layer(hp, bp, sp, jnp.zeros((bp, HALO_A, da), F32), wts_p, False)
            hs, tail_s, v_s = _even_layer(hs, bs, ss, _pad_rows_front(state_conv_a[p], HALO_A), wts_s, True)
            ca_p.append(tail_p)
            ca_s.append(tail_s)
            gv_s.append(v_s)
        else:
            w_in = w_in_odd[p]
            n_main = 3 * dc + rq + rkv
            w_in = jnp.concatenate([w_in, _rot_cols(w_in[:, n_main:])], axis=1).astype(BF16)
            wq = w_uq[p].reshape(rq, N_HEADS, NOPE_DIM + ROPE_DIM)
            wq = jnp.concatenate([wq, _rot_cols(wq[..., NOPE_DIM:])], axis=-1).reshape(rq, N_HEADS * HEAD_PAD)
            wq_t = wq.T.astype(BF16)
            wkv = w_ukv[p].reshape(rkv, N_HEADS, NOPE_DIM + V_DIM)
            wk = wkv[..., :NOPE_DIM].reshape(rkv, N_HEADS * NOPE_DIM).astype(BF16)
            wv_t = wkv[..., NOPE_DIM:].reshape(rkv, N_HEADS * V_DIM).T.astype(BF16)
            w_out = w_out_odd[p].astype(BF16)
            wts = (norm_mix[l], w_in, conv_c_w[p], q_norm_g[p].reshape(1, rq), wq_t,
                   kv_norm_g[p].reshape(1, rkv), wk, wv_t, w_out[:dc], w_out[dc:])
            hp, tail_p, c_p, r_p = _odd_layer(hp, bp, sp, jnp.zeros((bp, HALO_C, dc), F32), None, None,
                                              wts, cs_p, cs_p_q)
            hs, tail_s, c_s, r_s = _odd_layer(hs, bs, ss, _pad_rows_front(state_conv_c[p], HALO_C),
                                              cache_kv_latent[p], cache_k_rope[p], wts, cs_s, cs_s_q)
            cc_p.append(tail_p)
            cc_s.append(tail_s)
            lat_p.append(c_p)
            kr_p.append(r_p)
            lat_s.append(c_s)
            kr_s.append(r_s)
        wu, wd = w_ffn_up[l].astype(BF16), w_ffn_down[l].astype(BF16)
        last = l == depth - 1
        hp = _ffn(hp, norm_ffn[l], wu, wd, norm_final, last)
        hs = _ffn(hs, norm_ffn[l], wu, wd, norm_final, last)

    return (hp.reshape(bp, sp, d), hs.reshape(bs, ss, d),
            jnp.stack(ca_p), jnp.stack(ca_s), jnp.stack(gv_s), jnp.stack(cc_p), jnp.stack(cc_s),
            jnp.stack(lat_p), jnp.stack(kr_p), jnp.stack(lat_s), jnp.stack(kr_s))
```

```python
import functools
import math

import jax
import jax.numpy as jnp
from jax import lax
from jax.experimental import pallas as pl
from jax.experimental.pallas import tpu as pltpu

F32 = jnp.float32
BF16 = jnp.bfloat16

CHUNK = 64
CHUNK_SHIFT = 6
CONV_A = 31
CONV_C = 3
G_B = 8
GMLP_CHUNK = 128
N_HEADS = 16
NOPE_DIM = 128
ROPE_DIM = 64
V_DIM = 128
V_ROWS = V_DIM + 16
ROPE_THETA = 10000.0
EPS = 1e-6

LANES = 128
SUBLANES = 8
HEAD_PAD = 2 * LANES
HALO_A = 32
HALO_C = 8
MASK_VALUE = -1e30
VMEM_LIMIT = 56 * 1024 * 1024

TM_MATMUL = 1024
TN_MATMUL = 1024
TF_FFN = 512
TS_EVEN = 128
TS_ODD = 512
TQ_ATT = 512
TK_ATT = 256
CAST_BLOCK = (2048, 1024)
ATT_GROUP = 4
KV_ROWS_STEP = 1024


def _params(*sem):
    return pltpu.CompilerParams(dimension_semantics=sem, vmem_limit_bytes=VMEM_LIMIT)


def _rms(x, g):
    return x * lax.rsqrt(jnp.mean(x * x, axis=-1, keepdims=True) + EPS) * g


def _layer_norm(x, g, b):
    xc = x - jnp.mean(x, axis=-1, keepdims=True)
    return xc * lax.rsqrt(jnp.mean(xc * xc, axis=-1, keepdims=True) + EPS) * g + b


def _gelu(x):
    return 0.5 * x * (1.0 + lax.erf(x * math.sqrt(0.5)))


def _tile(n, t):
    t = min(n, t)
    assert n % t == 0, (n, t)
    return t


def _norm_matmul_body(x_ref, g_ref, w_ref, o_ref, xn_ref):
    @pl.when(pl.program_id(1) == 0)
    def _():
        xn_ref[...] = _rms(x_ref[...], g_ref[...]).astype(BF16)

    o_ref[...] = jnp.dot(xn_ref[...], w_ref[...], preferred_element_type=F32)


def _norm_matmul(x, g, w, tn):
    t, d = x.shape
    n = w.shape[1]
    tm = _tile(t, TM_MATMUL)
    return pl.pallas_call(
        _norm_matmul_body,
        grid=(t // tm, n // tn),
        in_specs=[pl.BlockSpec((tm, d), lambda i, j: (i, 0)),
                  pl.BlockSpec((1, d), lambda i, j: (0, 0)),
                  pl.BlockSpec((d, tn), lambda i, j: (0, j))],
        out_specs=pl.BlockSpec((tm, tn), lambda i, j: (i, j)),
        out_shape=jax.ShapeDtypeStruct((t, n), F32),
        scratch_shapes=[pltpu.VMEM((tm, d), BF16)],
        compiler_params=_params("parallel", "arbitrary"),
        name="norm_matmul",
    )(x, g.reshape(1, d), w)


def _even_mid_body(*refs, ts, da, gl, want_v):
    (zc_ref, zh_ref, hist_ref, cw_ref, cb_ref, lag_ref, lab_ref, lvg_ref, lvb_ref,
     wsp_ref, bsp_ref, cat_ref, tail_ref) = refs[:13]
    v_ref = refs[13] if want_v else None
    buf_ref, conv_ref = refs[-2:]
    i = pl.program_id(1)

    buf_ref[HALO_A:HALO_A + ts, :] = zc_ref[0, :, 0:da] * jax.nn.sigmoid(zc_ref[0, :, da:2 * da])

    @pl.when(i == 0)
    def _():
        buf_ref[0:HALO_A, :] = hist_ref[0]

    @pl.when(i > 0)
    def _():
        buf_ref[0:HALO_A, :] = zh_ref[0, :, 0:da] * jax.nn.sigmoid(zh_ref[0, :, da:2 * da])

    tail_ref[0] = buf_ref[ts:ts + HALO_A, :]

    rc = min(ts, 64)
    base = HALO_A - (CONV_A - 1)
    win = rc + HALO_A
    for c in range(da // LANES):
        cs = slice(c * LANES, (c + 1) * LANES)
        for r0 in range(0, ts, rc):
            window = buf_ref[r0:r0 + win, cs]
            acc = jnp.broadcast_to(cb_ref[:, cs], (rc, LANES))
            for sh in range(SUBLANES):
                rolled = window if sh == 0 else pltpu.roll(window, win - sh, axis=0)
                for off in range(sh, HALO_A + 1, SUBLANES):
                    j = off - base
                    if 0 <= j < CONV_A:
                        acc = acc + cw_ref[j:j + 1, cs] * rolled[off - sh:off - sh + rc]
            conv_ref[r0:r0 + rc, cs] = acc

    y = _layer_norm(conv_ref[...], lag_ref[...], lab_ref[...])
    cat_ref[0, :, 0:da] = (y * jax.nn.sigmoid(y)).astype(BF16)

    u = _gelu(zc_ref[0, :, 2 * da:3 * da])
    v = _layer_norm(_gelu(zc_ref[0, :, 3 * da:4 * da]), lvg_ref[...], lvb_ref[...])
    if want_v:
        v_ref[0] = v
    vb = v.astype(BF16)
    dh = da // G_B
    tril = lax.broadcasted_iota(jnp.int32, (gl, gl), 0) >= lax.broadcasted_iota(jnp.int32, (gl, gl), 1)
    for g in range(G_B):
        wg = jnp.where(tril, wsp_ref[g], 0.0).astype(BF16)
        gs = slice(g * dh, (g + 1) * dh)
        for k in range(ts // gl):
            rs = slice(k * gl, (k + 1) * gl)
            mixed = jnp.dot(wg, vb[rs, gs], preferred_element_type=F32) + bsp_ref[:, g:g + 1]
            cat_ref[0, rs, da + g * dh:da + (g + 1) * dh] = (u[rs, gs] * mixed).astype(BF16)


def _even_mid(z3, hist, conv_w, conv_b, lag, lab, lvg, lvb, w_sp, b_sp_t, want_v):
    b, s, n4 = z3.shape
    da = n4 // 4
    gl = w_sp.shape[-1]
    ts = _tile(s, TS_EVEN)
    assert ts % gl == 0 and ts % HALO_A == 0
    hb = ts // HALO_A
    row = lambda bi, i: (0, 0)
    out_shape = [jax.ShapeDtypeStruct((b, s, 2 * da), BF16), jax.ShapeDtypeStruct((b, HALO_A, da), F32)]
    out_specs = [pl.BlockSpec((1, ts, 2 * da), lambda bi, i: (bi, i, 0)),
                 pl.BlockSpec((1, HALO_A, da), lambda bi, i: (bi, 0, 0))]
    if want_v:
        out_shape.append(jax.ShapeDtypeStruct((b, s, da), F32))
        out_specs.append(pl.BlockSpec((1, ts, da), lambda bi, i: (bi, i, 0)))
    return pl.pallas_call(
        functools.partial(_even_mid_body, ts=ts, da=da, gl=gl, want_v=want_v),
        grid=(b, s // ts),
        in_specs=[pl.BlockSpec((1, ts, n4), lambda bi, i: (bi, i, 0)),
                  pl.BlockSpec((1, HALO_A, 2 * da), lambda bi, i: (bi, jnp.maximum(i * hb - 1, 0), 0)),
                  pl.BlockSpec((1, HALO_A, da), lambda bi, i: (bi, 0, 0)),
                  pl.BlockSpec((HALO_A, da), row),
                  pl.BlockSpec((1, da), row), pl.BlockSpec((1, da), row), pl.BlockSpec((1, da), row),
                  pl.BlockSpec((1, da), row), pl.BlockSpec((1, da), row),
                  pl.BlockSpec((G_B, gl, gl), lambda bi, i: (0, 0, 0)),
                  pl.BlockSpec((gl, G_B), row)],
        out_specs=out_specs,
        out_shape=out_shape,
        scratch_shapes=[pltpu.VMEM((HALO_A + ts, da), F32), pltpu.VMEM((ts, da), F32)],
        compiler_params=_params("parallel", "arbitrary"),
        name="even_mid",
    )(z3, z3, hist, conv_w, conv_b, lag, lab, lvg, lvb, w_sp, b_sp_t)


def _proj_residual_body(*refs, n_in):
    x_refs, w_refs = refs[:n_in], refs[n_in:2 * n_in]
    r_ref, o_ref = refs[2 * n_in], refs[2 * n_in + 1]
    acc = r_ref[...]
    for x_ref, w_ref in zip(x_refs, w_refs):
        acc = acc + jnp.dot(x_ref[...], w_ref[...], preferred_element_type=F32)
    o_ref[...] = acc


def _proj_residual(xs, ws, res):
    t, n = res.shape
    tm = _tile(t, TM_MATMUL)
    tn = _tile(n, TN_MATMUL)
    in_specs = [pl.BlockSpec((tm, x.shape[1]), lambda i, j: (i, 0)) for x in xs]
    in_specs += [pl.BlockSpec((w.shape[0], tn), lambda i, j: (0, j)) for w in ws]
    in_specs.append(pl.BlockSpec((tm, tn), lambda i, j: (i, j)))
    return pl.pallas_call(
        functools.partial(_proj_residual_body, n_in=len(xs)),
        grid=(t // tm, n // tn),
        in_specs=in_specs,
        out_specs=pl.BlockSpec((tm, tn), lambda i, j: (i, j)),
        out_shape=jax.ShapeDtypeStruct((t, n), F32),
        compiler_params=_params("parallel", "arbitrary"),
        name="proj_residual",
    )(*xs, *ws, res)


def _ffn_body(x_ref, g_ref, wu_ref, wd_ref, gf_ref, o_ref, xn_ref, *, final_norm):
    j = pl.program_id(1)

    @pl.when(j == 0)
    def _():
        x = x_ref[...]
        xn_ref[...] = _rms(x, g_ref[...]).astype(BF16)
        o_ref[...] = x

    a = jnp.maximum(jnp.dot(xn_ref[...], wu_ref[...], preferred_element_type=F32), 0.0)
    o_ref[...] += jnp.dot((a * a).astype(BF16), wd_ref[...], preferred_element_type=F32)

    if final_norm:
        @pl.when(j == pl.num_programs(1) - 1)
        def _():
            o_ref[...] = _rms(o_ref[...], gf_ref[...])


def _cast_bf16_body(w_ref, o_ref):
    o_ref[...] = w_ref[...].astype(BF16)


def _cast_bf16(w):
    nl, r, c = w.shape
    tr, tc = _tile(r, CAST_BLOCK[0]), _tile(c, CAST_BLOCK[1])
    spec = pl.BlockSpec((1, tr, tc), lambda l, i, j: (l, i, j))
    return pl.pallas_call(
        _cast_bf16_body,
        grid=(nl, r // tr, c // tc),
        in_specs=[spec],
        out_specs=spec,
        out_shape=jax.ShapeDtypeStruct(w.shape, BF16),
        compiler_params=_params("parallel", "parallel", "parallel"),
        name="cast_bf16",
    )(w)


def _ffn(x, g, wu, wd, layer, gf, final_norm):
    t, d = x.shape
    f = wu.shape[2]
    tm = _tile(t, TM_MATMUL)
    tf = _tile(f, TF_FFN)
    return pl.pallas_call(
        functools.partial(_ffn_body, final_norm=final_norm),
        grid=(t // tm, f // tf),
        in_specs=[pl.BlockSpec((tm, d), lambda i, j: (i, 0)),
                  pl.BlockSpec((1, d), lambda i, j: (0, 0)),
                  pl.BlockSpec((None, d, tf), lambda i, j: (layer, 0, j)),
                  pl.BlockSpec((None, tf, d), lambda i, j: (layer, j, 0)),
                  pl.BlockSpec((1, d), lambda i, j: (0, 0))],
        out_specs=pl.BlockSpec((tm, d), lambda i, j: (i, 0)),
        out_shape=jax.ShapeDtypeStruct((t, d), F32),
        scratch_shapes=[pltpu.VMEM((tm, d), BF16)],
        compiler_params=_params("parallel", "arbitrary"),
        name="ffn",
    )(x, g.reshape(1, d), wu, wd, gf.reshape(1, d))


def _rope_fold(y, cs):
    t = y * cs
    return t + pltpu.roll(t, ROPE_DIM, axis=1)


def _odd_mid_body(zc_ref, hc_ref, hx_ref, hist_ref, cw_ref, qg_ref, kvg_ref, cs_ref,
                  cout_ref, qn_ref, ckv_ref, kr_ref, tail_ref, buf_ref, *, ts, dc, rq, rkv):
    i = pl.program_id(1)
    gx = zc_ref[0, :, dc:2 * dc] * zc_ref[0, :, 2 * dc:3 * dc]
    buf_ref[HALO_C:HALO_C + ts, :] = gx

    @pl.when(i == 0)
    def _():
        buf_ref[0:HALO_C, :] = hist_ref[0]

    @pl.when(i > 0)
    def _():
        buf_ref[0:HALO_C, :] = hc_ref[0] * hx_ref[0]

    tail_ref[0] = buf_ref[ts:ts + HALO_C, :]
    conv = cw_ref[CONV_C - 1:CONV_C, :] * gx
    for j in range(CONV_C - 1):
        off = HALO_C - (CONV_C - 1) + j
        conv = conv + cw_ref[j:j + 1, :] * buf_ref[off:off + ts, :]
    cout_ref[0] = (zc_ref[0, :, 0:dc] * conv).astype(BF16)

    o = 3 * dc
    qn_ref[0] = _rms(zc_ref[0, :, o:o + rq], qg_ref[...]).astype(BF16)
    ckv_ref[0] = _rms(zc_ref[0, :, o + rq:o + rq + rkv], kvg_ref[...])
    kr = _rope_fold(zc_ref[0, :, o + rq + rkv:o + rq + rkv + LANES], cs_ref[...])
    lane = lax.broadcasted_iota(jnp.int32, kr.shape, 1)
    kr_ref[0] = jnp.where(lane < ROPE_DIM, kr, 0.0)


def _odd_mid(z3, hist, conv_w, qg, kvg, cs_tab, dc, rq, rkv):
    b, s, n = z3.shape
    ts = _tile(s, TS_ODD)
    assert ts % HALO_C == 0 and n == 3 * dc + rq + rkv + LANES and dc % 1024 == 0
    hb = ts // HALO_C
    halo = lambda col: pl.BlockSpec((1, HALO_C, dc), lambda bi, i: (bi, jnp.maximum(i * hb - 1, 0), col))
    row = lambda bi, i: (0, 0)
    tok = lambda w: pl.BlockSpec((1, ts, w), lambda bi, i: (bi, i, 0))
    return pl.pallas_call(
        functools.partial(_odd_mid_body, ts=ts, dc=dc, rq=rq, rkv=rkv),
        grid=(b, s // ts),
        in_specs=[tok(n), halo(1), halo(2),
                  pl.BlockSpec((1, HALO_C, dc), lambda bi, i: (bi, 0, 0)),
                  pl.BlockSpec((CONV_C, dc), row),
                  pl.BlockSpec((1, rq), row), pl.BlockSpec((1, rkv), row),
                  pl.BlockSpec((ts, LANES), lambda bi, i: (i, 0))],
        out_specs=[tok(dc), tok(rq), tok(rkv), tok(LANES),
                   pl.BlockSpec((1, HALO_C, dc), lambda bi, i: (bi, 0, 0))],
        out_shape=[jax.ShapeDtypeStruct((b, s, dc), BF16),
                   jax.ShapeDtypeStruct((b, s, rq), BF16),
                   jax.ShapeDtypeStruct((b, s, rkv), F32),
                   jax.ShapeDtypeStruct((b, s, LANES), F32),
                   jax.ShapeDtypeStruct((b, HALO_C, dc), F32)],
        scratch_shapes=[pltpu.VMEM((HALO_C + ts, dc), F32)],
        compiler_params=_params("parallel", "arbitrary"),
        name="odd_mid",
    )(z3, z3, z3, hist, conv_w, qg, kvg, cs_tab)


def _q_up_body(w_ref, qn_ref, cs_ref, o_ref, *, scale):
    acc = lax.dot_general(w_ref[...], qn_ref[...], (((1,), (1,)), ((), ())), preferred_element_type=F32)
    cs = cs_ref[...]
    for h in range(acc.shape[0] // HEAD_PAD):
        lo = h * HEAD_PAD
        o_ref[0, lo:lo + NOPE_DIM, :] = (acc[lo:lo + NOPE_DIM] * scale).astype(BF16)
        t = acc[lo + NOPE_DIM:lo + HEAD_PAD] * cs
        o_ref[0, lo + NOPE_DIM:lo + NOPE_DIM + ROPE_DIM, :] = ((t[:ROPE_DIM] + t[ROPE_DIM:]) * scale).astype(BF16)
        o_ref[0, lo + NOPE_DIM + ROPE_DIM:lo + HEAD_PAD, :] = jnp.zeros((ROPE_DIM, acc.shape[1]), BF16)


def _q_up(qn, w_t, cs_t, b, s, scale):
    r = qn.shape[1]
    n = w_t.shape[0]
    tm = _tile(s, TM_MATMUL)
    tn = _tile(n, TN_MATMUL)
    per = s // tm
    return pl.pallas_call(
        functools.partial(_q_up_body, scale=scale),
        grid=(b * per, n // tn),
        in_specs=[pl.BlockSpec((tn, r), lambda i, j: (j, 0)),
                  pl.BlockSpec((tm, r), lambda i, j: (i, 0)),
                  pl.BlockSpec((LANES, tm), lambda i, j: (0, i % per))],
        out_specs=pl.BlockSpec((1, tn, tm), lambda i, j: (i // per, j, i % per)),
        out_shape=jax.ShapeDtypeStruct((b, n, s), BF16),
        compiler_params=_params("parallel", "arbitrary"),
        name="q_up",
    )(w_t, qn, cs_t)


def _kv_up_body(c_ref, kr_ref, wk_ref, wvt_ref, k_ref, vt_ref):
    c = c_ref[...].astype(BF16)
    k = jnp.dot(c, wk_ref[...], preferred_element_type=F32)
    vt = lax.dot_general(wvt_ref[...], c, (((1,), (1,)), ((), ())), preferred_element_type=F32)
    kr = kr_ref[...].astype(BF16)
    tk = vt_ref.shape[-1]
    for h in range(k.shape[1] // NOPE_DIM):
        k_ref[:, h * HEAD_PAD:h * HEAD_PAD + NOPE_DIM] = k[:, h * NOPE_DIM:(h + 1) * NOPE_DIM].astype(BF16)
        k_ref[:, h * HEAD_PAD + NOPE_DIM:(h + 1) * HEAD_PAD] = kr
        for t in range(vt_ref.shape[2]):
            vt_ref[0, h, t, 0:V_DIM, :] = vt[h * V_DIM:(h + 1) * V_DIM, t * tk:(t + 1) * tk].astype(BF16)
            vt_ref[0, h, t, V_DIM:V_ROWS, :] = jnp.ones((V_ROWS - V_DIM, tk), BF16)


def _kv_up(c, kr, wk, wv_t, b, skv, tk):
    t, r = c.shape
    nkt = skv // tk
    tm = _tile(skv, KV_ROWS_STEP)
    kts = tm // tk
    per = skv // tm
    hpt = 4
    return pl.pallas_call(
        _kv_up_body,
        grid=(t // tm, N_HEADS // hpt),
        in_specs=[pl.BlockSpec((tm, r), lambda i, j: (i, 0)),
                  pl.BlockSpec((tm, LANES), lambda i, j: (i, 0)),
                  pl.BlockSpec((r, hpt * NOPE_DIM), lambda i, j: (0, j)),
                  pl.BlockSpec((hpt * V_DIM, r), lambda i, j: (j, 0))],
        out_specs=[pl.BlockSpec((tm, hpt * HEAD_PAD), lambda i, j: (i, j)),
                   pl.BlockSpec((1, hpt, kts, V_ROWS, tk), lambda i, j: (i // per, j, i % per, 0, 0))],
        out_shape=[jax.ShapeDtypeStruct((t, N_HEADS * HEAD_PAD), BF16),
                   jax.ShapeDtypeStruct((b, N_HEADS, nkt, V_ROWS, tk), BF16)],
        compiler_params=_params("parallel", "arbitrary"),
        name="kv_up",
    )(c, kr, wk, wv_t)


def _colmax(s):
    while s.shape[0] > 8 and s.shape[0] % 16 == 0:
        half = s.shape[0] // 2
        s = jnp.maximum(s[:half], s[half:])
    return jnp.max(s, axis=0, keepdims=True)


def _attention_body(qt_ref, k_ref, vt_ref, o_ref, s_ref, p_ref, acc_ref, *, tq, tk, nkt, past, t_valid):
    qt = qt_ref[0]
    q0 = past + pl.program_id(2) * tq
    see_all = jnp.minimum(((q0 >> CHUNK_SHIFT) + 1) * CHUNK, t_valid)
    see_any = jnp.minimum((((q0 + tq - 1) >> CHUNK_SHIFT) + 1) * CHUNK, t_valid)
    n_full = see_all // tk
    n_all = (see_any + tk - 1) // tk

    def k_tile(kt):
        rows = pl.ds(pl.multiple_of(jnp.minimum(kt, nkt - 1) * tk, tk), tk)
        return k_ref[0, rows, :]

    def half_step(kt, carry, slot, masked):
        m, alpha = carry
        other = 1 - slot
        pv = jnp.dot(vt_ref[0, 0, jnp.clip(kt - 1, 0, nkt - 1)], p_ref[other], preferred_element_type=F32)
        s_next = jnp.dot(k_tile(kt + 1), qt, preferred_element_type=F32)
        s = s_ref[slot]
        if masked:
            k_pos = kt * tk + lax.broadcasted_iota(jnp.int32, (tk, 1), 0)
            q_pos = q0 + lax.broadcasted_iota(jnp.int32, (1, tq), 1)
            k_chunk = jnp.where(k_pos < t_valid, k_pos >> CHUNK_SHIFT, jnp.iinfo(jnp.int32).max)
            s = jnp.where(k_chunk <= q_pos >> CHUNK_SHIFT, s, MASK_VALUE)
        m_new = jnp.maximum(m, _colmax(s))
        p_ref[slot] = jnp.exp2(s - m_new).astype(BF16)
        acc_ref[...] = alpha * acc_ref[...] + pv
        s_ref[other] = s_next
        return m_new, jnp.exp2(m - m_new)

    def group(i, carry, first, size, masked):
        for g in range(size):
            carry = half_step(first + size * i + g, carry, g % 2, masked)
        return carry

    s_ref[0] = jnp.dot(k_tile(0), qt, preferred_element_type=F32)
    p_ref[1] = jnp.zeros((tk, tq), BF16)
    acc_ref[...] = jnp.zeros((V_ROWS, tq), F32)
    init = (jnp.full((1, tq), MASK_VALUE, F32), jnp.ones((1, tq), F32))
    groups_full = n_full // ATT_GROUP
    carry = lax.fori_loop(0, groups_full,
                          functools.partial(group, first=0, size=ATT_GROUP, masked=False), init)
    first = groups_full * ATT_GROUP
    pairs = (n_all - first + 1) // 2
    _, alpha = lax.fori_loop(0, pairs, functools.partial(group, first=first, size=2, masked=True), carry)
    last = jnp.minimum(first + 2 * pairs - 1, nkt - 1)
    acc = alpha * acc_ref[...] + jnp.dot(vt_ref[0, 0, last], p_ref[1], preferred_element_type=F32)
    o_ref[0] = jnp.transpose(acc[:V_DIM] / acc[V_DIM:V_DIM + 1]).astype(BF16)


def _attention(qt, k3, vt, past, t_valid):
    b, _, sq = qt.shape
    skv = k3.shape[1]
    nkt, tk = vt.shape[2], vt.shape[4]
    tq = _tile(sq, TQ_ATT)
    return pl.pallas_call(
        functools.partial(_attention_body, tq=tq, tk=tk, nkt=nkt, past=past, t_valid=t_valid),
        scratch_shapes=[pltpu.VMEM((2, tk, tq), F32), pltpu.VMEM((2, tk, tq), BF16), pltpu.VMEM((V_ROWS, tq), F32)],
        grid=(b, N_HEADS, sq // tq),
        in_specs=[pl.BlockSpec((1, HEAD_PAD, tq), lambda bi, h, i: (bi, h, i)),
                  pl.BlockSpec((1, skv, HEAD_PAD), lambda bi, h, i: (bi, 0, h)),
                  pl.BlockSpec((1, 1, nkt, V_ROWS, tk), lambda bi, h, i: (bi, h, 0, 0, 0))],
        out_specs=pl.BlockSpec((1, tq, V_DIM), lambda bi, h, i: (bi, i, h)),
        out_shape=jax.ShapeDtypeStruct((b, sq, N_HEADS * V_DIM), BF16),
        compiler_params=_params("parallel", "parallel", "arbitrary"),
        name="attention",
    )(qt, k3, vt)


def _latent_attention_body(qn_ref, wq_ref, wuk_ref, cs_ref, klat_ref, wuv_ref, o_ref, q_ref, acc_ref,
                           *, ss, tk, n_tiles, past, t_valid, scale, rkv):
    qn = qn_ref[0]
    cs = cs_ref[...]
    lane = lax.broadcasted_iota(jnp.int32, (ss, LANES), 1)
    for h in range(N_HEADS):
        qh = jnp.dot(qn, wq_ref[:, h * HEAD_PAD:(h + 1) * HEAD_PAD], preferred_element_type=F32)
        q_abs = lax.dot_general(qh[:, :NOPE_DIM].astype(BF16), wuk_ref[:, h * NOPE_DIM:(h + 1) * NOPE_DIM],
                                (((1,), (1,)), ((), ())), preferred_element_type=F32)
        rope = jnp.where(lane < ROPE_DIM, _rope_fold(qh[:, NOPE_DIM:], cs), 0.0)
        q_ref[h * ss:(h + 1) * ss, 0:rkv] = (q_abs * scale).astype(BF16)
        q_ref[h * ss:(h + 1) * ss, rkv:rkv + LANES] = (rope * scale).astype(BF16)

    q = q_ref[...]
    cols = N_HEADS * ss
    q_chunk = (past + lax.broadcasted_iota(jnp.int32, (1, cols), 1) % ss) >> CHUNK_SHIFT
    acc_ref[...] = jnp.zeros((rkv, cols), F32)

    def step(kt, carry):
        m, l = carry
        tile = klat_ref[0, pl.ds(pl.multiple_of(kt * tk, tk), tk), :]
        s = lax.dot_general(tile, q, (((1,), (1,)), ((), ())), preferred_element_type=F32)
        k_pos = kt * tk + lax.broadcasted_iota(jnp.int32, (tk, 1), 0)
        k_chunk = jnp.where(k_pos < t_valid, k_pos >> CHUNK_SHIFT, jnp.iinfo(jnp.int32).max)
        s = jnp.where(k_chunk <= q_chunk, s, MASK_VALUE)
        m_new = jnp.maximum(m, _colmax(s))
        alpha = jnp.exp2(m - m_new)
        p = jnp.exp2(s - m_new)
        pv = lax.dot_general(tile[:, :rkv], p.astype(BF16), (((0,), (0,)), ((), ())), preferred_element_type=F32)
        acc_ref[...] = alpha * acc_ref[...] + pv
        return m_new, alpha * l + jnp.sum(p, axis=0, keepdims=True)

    init = (jnp.full((1, cols), MASK_VALUE, F32), jnp.zeros((1, cols), F32))
    _, l = lax.fori_loop(0, n_tiles, step, init)
    o_lat = jnp.transpose(acc_ref[...] / l).astype(BF16)
    for h in range(N_HEADS):
        o_ref[0, :, h * V_DIM:(h + 1) * V_DIM] = jnp.dot(
            o_lat[h * ss:(h + 1) * ss], wuv_ref[:, h * V_DIM:(h + 1) * V_DIM],
            preferred_element_type=F32).astype(BF16)


def _latent_attention(qn, wq, wuk, cs_tab, klat, wuv, past, t_valid, scale):
    b, ss, rq = qn.shape
    skv, feat = klat.shape[1], klat.shape[2]
    rkv = feat - LANES
    tk = min(TK_ATT, skv)
    see_any = min((((past + ss - 1) >> CHUNK_SHIFT) + 1) * CHUNK, t_valid)
    n_tiles = -(-see_any // tk)
    assert skv % tk == 0 and n_tiles * tk <= skv
    whole = lambda a: pl.BlockSpec(a.shape, lambda bi: (0,) * a.ndim)
    return pl.pallas_call(
        functools.partial(_latent_attention_body, ss=ss, tk=tk, n_tiles=n_tiles, past=past, t_valid=t_valid,
                          scale=scale, rkv=rkv),
        grid=(b,),
        in_specs=[pl.BlockSpec((1, ss, rq), lambda bi: (bi, 0, 0)), whole(wq), whole(wuk), whole(cs_tab),
                  pl.BlockSpec((1, skv, feat), lambda bi: (bi, 0, 0)), whole(wuv)],
        out_specs=pl.BlockSpec((1, ss, N_HEADS * V_DIM), lambda bi: (bi, 0, 0)),
        out_shape=jax.ShapeDtypeStruct((b, ss, N_HEADS * V_DIM), BF16),
        scratch_shapes=[pltpu.VMEM((N_HEADS * ss, feat), BF16), pltpu.VMEM((rkv, N_HEADS * ss), F32)],
        compiler_params=_params("parallel"),
        name="latent_attention",
    )(qn, wq, wuk, cs_tab, klat, wuv)


def _rope_table(past, s):
    half = ROPE_DIM // 2
    inv = ROPE_THETA ** (-jnp.arange(half, dtype=F32) / half)
    ang = (past + jnp.arange(s, dtype=jnp.int32)).astype(F32)[:, None] * inv[None, :]
    cos, sin = jnp.cos(ang), jnp.sin(ang)
    return jnp.concatenate([cos, cos, sin, sin], axis=-1)


def _rot_cols(w):
    half = ROPE_DIM // 2
    return jnp.concatenate([-w[..., half:], w[..., :half]], axis=-1)


def _pad_rows_front(x, rows):
    return jnp.pad(x, ((0, 0), (rows - x.shape[1], 0), (0, 0)))


def _even_layer(h, b, s, hist, wts, want_v):
    (g_mix, w_in, conv_w, conv_b, lag, lab, lvg, lvb, w_sp, b_sp_t, w_out) = wts
    z = _norm_matmul(h, g_mix, w_in, _tile(w_in.shape[1], TN_MATMUL))
    outs = _even_mid(z.reshape(b, s, -1), hist, conv_w, conv_b, lag, lab, lvg, lvb, w_sp, b_sp_t, want_v)
    cat, tail = outs[0], outs[1]
    h = _proj_residual([cat.reshape(b * s, -1)], [w_out], h)
    return h, tail[:, HALO_A - (CONV_A - 1):], (outs[2] if want_v else None)


def _odd_layer(h, b, s, hist, lat_prev, kr_prev, wts, cs_tab):
    (g_mix, w_in, conv_w, qg, wq, kvg, w_k, w_v, w_out_c, w_out_a) = wts
    dc = conv_w.shape[1]
    rq, rkv = qg.shape[1], kvg.shape[1]
    z = _norm_matmul(h, g_mix, w_in, w_in.shape[1] // 3)
    cout, qn, ckv, kr, tail = _odd_mid(z.reshape(b, s, -1), hist, conv_w, qg, kvg, cs_tab, dc, rq, rkv)
    scale = math.log2(math.e) / math.sqrt(NOPE_DIM + ROPE_DIM)
    if lat_prev is None:
        assert s % LANES == 0
        tk = min(TK_ATT, s)
        qt = _q_up(qn.reshape(b * s, rq), wq.T, cs_tab.T, b, s, scale)
        k, vt = _kv_up(ckv.reshape(b * s, rkv), kr.reshape(b * s, LANES), w_k, w_v.T, b, s, tk)
        attn = _attention(qt, k.reshape(b, s, -1), vt, 0, s)
    else:
        past = lat_prev.shape[1]
        t_valid = past + s
        tk = min(TK_ATT, t_valid)
        skv = -(-t_valid // tk) * tk
        kr_prev = jnp.pad(kr_prev, ((0, 0), (0, 0), (0, LANES - ROPE_DIM)))
        klat = jnp.concatenate([jnp.concatenate([lat_prev, ckv], axis=1),
                                jnp.concatenate([kr_prev, kr], axis=1)], axis=-1).astype(BF16)
        klat = jnp.pad(klat, ((0, 0), (0, skv - t_valid), (0, 0)))
        attn = _latent_attention(qn, wq, w_k, cs_tab, klat, w_v, past, t_valid, scale)
    h = _proj_residual([cout.reshape(b * s, dc), attn.reshape(b * s, -1)], [w_out_c, w_out_a], h)
    return h, tail[:, HALO_C - (CONV_C - 1):], ckv, kr[..., :ROPE_DIM]


def kernel(x_prompt, x_sample, state_conv_a, state_conv_c, cache_kv_latent, cache_k_rope, norm_mix, norm_ffn, norm_final, w_in_even, conv_a_w, conv_a_b, ln_a_g, ln_a_b, ln_v_g, ln_v_b, w_spatial, b_spatial, w_out_even, w_in_odd, conv_c_w, q_norm_g, w_uq, kv_norm_g, w_ukv, w_out_odd, w_ffn_up, w_ffn_down):
    bp, sp, d = x_prompt.shape
    bs, ss, _ = x_sample.shape
    depth = norm_mix.shape[0]
    da = conv_a_w.shape[-1]
    dc = conv_c_w.shape[-1]
    rq, rkv = q_norm_g.shape[-1], kv_norm_g.shape[-1]
    past = cache_kv_latent.shape[2]
    assert sp % GMLP_CHUNK == 0 and ss <= GMLP_CHUNK and ss >= HALO_A and past % GMLP_CHUNK == 0

    hp = x_prompt.reshape(bp * sp, d)
    hs = x_sample.reshape(bs * ss, d)
    cs_p = _rope_table(0, sp)
    cs_s = _rope_table(past, ss)
    wu_all, wd_all = _cast_bf16(w_ffn_up), _cast_bf16(w_ffn_down)

    ca_p, ca_s, gv_s, cc_p, cc_s, lat_p, kr_p, lat_s, kr_s = [], [], [], [], [], [], [], [], []
    for l in range(depth):
        p = l // 2
        if l % 2 == 0:
            row = lambda a: a[p].reshape(1, -1)
            gl_p, gl_s = min(sp, GMLP_CHUNK), min(ss, GMLP_CHUNK)
            common = (norm_mix[l], w_in_even[p].astype(BF16),
                      jnp.pad(conv_a_w[p], ((0, HALO_A - CONV_A), (0, 0))), row(conv_a_b),
                      row(ln_a_g), row(ln_a_b), row(ln_v_g), row(ln_v_b))
            w_out = w_out_even[p].astype(BF16)
            wts_p = common + (w_spatial[p][:, :gl_p, :gl_p], b_spatial[p][:, :gl_p].T, w_out)
            wts_s = common + (w_spatial[p][:, :gl_s, :gl_s], b_spatial[p][:, :gl_s].T, w_out)
            hp, tail_p, _ = _even_layer(hp, bp, sp, jnp.zeros((bp, HALO_A, da), F32), wts_p, False)
            hs, tail_s, v_s = _even_layer(hs, bs, ss, _pad_rows_front(state_conv_a[p], HALO_A), wts_s, True)
            ca_p.append(tail_p)
            ca_s.append(tail_s)
            gv_s.append(v_s)
        else:
            w_in = w_in_odd[p]
            n_main = 3 * dc + rq + rkv
            w_in = jnp.concatenate([w_in, _rot_cols(w_in[:, n_main:])], axis=1).astype(BF16)
            wq = w_uq[p].reshape(rq, N_HEADS, NOPE_DIM + ROPE_DIM)
            wq = jnp.concatenate([wq, _rot_cols(wq[..., NOPE_DIM:])], axis=-1).reshape(rq, N_HEADS * HEAD_PAD)
            wq = wq.astype(BF16)
            wkv = w_ukv[p].reshape(rkv, N_HEADS, NOPE_DIM + V_DIM)
            wk = wkv[..., :NOPE_DIM].reshape(rkv, N_HEADS * NOPE_DIM).astype(BF16)
            wv = wkv[..., NOPE_DIM:].reshape(rkv, N_HEADS * V_DIM).astype(BF16)
            w_out = w_out_odd[p].astype(BF16)
            wts = (norm_mix[l], w_in, conv_c_w[p], q_norm_g[p].reshape(1, rq), wq,
                   kv_norm_g[p].reshape(1, rkv), wk, wv, w_out[:dc], w_out[dc:])
            hp, tail_p, c_p, r_p = _odd_layer(hp, bp, sp, jnp.zeros((bp, HALO_C, dc), F32), None, None,
                                              wts, cs_p)
            hs, tail_s, c_s, r_s = _odd_layer(hs, bs, ss, _pad_rows_front(state_conv_c[p], HALO_C),
                                              cache_kv_latent[p], cache_k_rope[p], wts, cs_s)
            cc_p.append(tail_p)
            cc_s.append(tail_s)
            lat_p.append(c_p)
            kr_p.append(r_p)
            lat_s.append(c_s)
            kr_s.append(r_s)
        last = l == depth - 1
        hp = _ffn(hp, norm_ffn[l], wu_all, wd_all, l, norm_final, last)
        hs = _ffn(hs, norm_ffn[l], wu_all, wd_all, l, norm_final, last)

    return (hp.reshape(bp, sp, d), hs.reshape(bs, ss, d),
            jnp.stack(ca_p), jnp.stack(ca_s), jnp.stack(gv_s), jnp.stack(cc_p), jnp.stack(cc_s),
            jnp.stack(lat_p), jnp.stack(kr_p), jnp.stack(lat_s), jnp.stack(kr_s))
```

```python
import functools
import math

import jax
import jax.numpy as jnp
from jax import lax
from jax.experimental import pallas as pl
from jax.experimental.pallas import tpu as pltpu

F32 = jnp.float32
BF16 = jnp.bfloat16

CHUNK = 64
CHUNK_SHIFT = 6
CONV_A = 31
CONV_C = 3
G_B = 8
GMLP_CHUNK = 128
N_HEADS = 16
NOPE_DIM = 128
ROPE_DIM = 64
V_DIM = 128
V_ROWS = V_DIM + 16
ROPE_THETA = 10000.0
EPS = 1e-6

LANES = 128
SUBLANES = 8
HEAD_PAD = 2 * LANES
HALO_A = 32
HALO_C = 8
MASK_VALUE = -1e30
VMEM_LIMIT = 56 * 1024 * 1024

TM_MATMUL = 1024
TN_MATMUL = 1024
TF_FFN = 512
TS_EVEN = 128
TS_ODD = 512
TQ_ATT = 512
TK_ATT = 256
CAST_BLOCK = (1024, 2048)
ATT_GROUP = 4
KV_ROWS_STEP = 1024


def _params(*sem):
    return pltpu.CompilerParams(dimension_semantics=sem, vmem_limit_bytes=VMEM_LIMIT)


def _rms(x, g):
    return x * lax.rsqrt(jnp.mean(x * x, axis=-1, keepdims=True) + EPS) * g


def _layer_norm(x, g, b):
    xc = x - jnp.mean(x, axis=-1, keepdims=True)
    return xc * lax.rsqrt(jnp.mean(xc * xc, axis=-1, keepdims=True) + EPS) * g + b


def _sigmoid(x):
    return 0.5 * jnp.tanh(0.5 * x) + 0.5


def _gelu(x):
    return 0.5 * x * (1.0 + lax.erf(x * math.sqrt(0.5)))


def _tile(n, t):
    t = min(n, t)
    assert n % t == 0, (n, t)
    return t


def _norm_matmul_body(x_ref, g_ref, w_ref, o_ref, xn_ref):
    @pl.when(pl.program_id(1) == 0)
    def _():
        xn_ref[...] = _rms(x_ref[...], g_ref[...]).astype(BF16)

    o_ref[...] = jnp.dot(xn_ref[...], w_ref[...], preferred_element_type=F32)


def _norm_matmul(x, g, w, tn):
    t, d = x.shape
    n = w.shape[1]
    tm = _tile(t, TM_MATMUL)
    return pl.pallas_call(
        _norm_matmul_body,
        grid=(t // tm, n // tn),
        in_specs=[pl.BlockSpec((tm, d), lambda i, j: (i, 0)),
                  pl.BlockSpec((1, d), lambda i, j: (0, 0)),
                  pl.BlockSpec((d, tn), lambda i, j: (0, j))],
        out_specs=pl.BlockSpec((tm, tn), lambda i, j: (i, j)),
        out_shape=jax.ShapeDtypeStruct((t, n), F32),
        scratch_shapes=[pltpu.VMEM((tm, d), BF16)],
        compiler_params=_params("parallel", "arbitrary"),
        name="norm_matmul",
    )(x, g.reshape(1, d), w)


def _even_mid_body(*refs, ts, da, gl, want_v):
    (zc_ref, zh_ref, hist_ref, cw_ref, cb_ref, lag_ref, lab_ref, lvg_ref, lvb_ref,
     wsp_ref, bsp_ref, cat_ref, tail_ref) = refs[:13]
    v_ref = refs[13] if want_v else None
    buf_ref, conv_ref = refs[-2:]
    i = pl.program_id(1)

    buf_ref[HALO_A:HALO_A + ts, :] = zc_ref[0, :, 0:da] * _sigmoid(zc_ref[0, :, da:2 * da])

    @pl.when(i == 0)
    def _():
        buf_ref[0:HALO_A, :] = hist_ref[0]

    @pl.when(i > 0)
    def _():
        buf_ref[0:HALO_A, :] = zh_ref[0, :, 0:da] * _sigmoid(zh_ref[0, :, da:2 * da])

    tail_ref[0] = buf_ref[ts:ts + HALO_A, :]

    rc = min(ts, 128)
    base = HALO_A - (CONV_A - 1)
    win = rc + HALO_A
    for c in range(da // LANES):
        cs = slice(c * LANES, (c + 1) * LANES)
        for r0 in range(0, ts, rc):
            window = buf_ref[r0:r0 + win, cs]
            acc = jnp.broadcast_to(cb_ref[:, cs], (rc, LANES))
            for sh in range(SUBLANES):
                rolled = window if sh == 0 else pltpu.roll(window, win - sh, axis=0)
                for off in range(sh, HALO_A + 1, SUBLANES):
                    j = off - base
                    if 0 <= j < CONV_A:
                        acc = acc + cw_ref[j:j + 1, cs] * rolled[off - sh:off - sh + rc]
            conv_ref[r0:r0 + rc, cs] = acc

    y = _layer_norm(conv_ref[...], lag_ref[...], lab_ref[...])
    cat_ref[0, :, 0:da] = (y * _sigmoid(y)).astype(BF16)

    u = _gelu(zc_ref[0, :, 2 * da:3 * da])
    v = _layer_norm(_gelu(zc_ref[0, :, 3 * da:4 * da]), lvg_ref[...], lvb_ref[...])
    if want_v:
        v_ref[0] = v
    vb = v.astype(BF16)
    dh = da // G_B
    tril = lax.broadcasted_iota(jnp.int32, (gl, gl), 0) >= lax.broadcasted_iota(jnp.int32, (gl, gl), 1)
    for g in range(G_B):
        wg = jnp.where(tril, wsp_ref[g], 0.0).astype(BF16)
        gs = slice(g * dh, (g + 1) * dh)
        for k in range(ts // gl):
            rs = slice(k * gl, (k + 1) * gl)
            mixed = jnp.dot(wg, vb[rs, gs], preferred_element_type=F32) + bsp_ref[:, g:g + 1]
            cat_ref[0, rs, da + g * dh:da + (g + 1) * dh] = (u[rs, gs] * mixed).astype(BF16)


def _even_mid(z3, hist, conv_w, conv_b, lag, lab, lvg, lvb, w_sp, b_sp_t, want_v):
    b, s, n4 = z3.shape
    da = n4 // 4
    gl = w_sp.shape[-1]
    ts = _tile(s, TS_EVEN)
    assert ts % gl == 0 and ts % HALO_A == 0
    hb = ts // HALO_A
    row = lambda bi, i: (0, 0)
    out_shape = [jax.ShapeDtypeStruct((b, s, 2 * da), BF16), jax.ShapeDtypeStruct((b, HALO_A, da), F32)]
    out_specs = [pl.BlockSpec((1, ts, 2 * da), lambda bi, i: (bi, i, 0)),
                 pl.BlockSpec((1, HALO_A, da), lambda bi, i: (bi, 0, 0))]
    if want_v:
        out_shape.append(jax.ShapeDtypeStruct((b, s, da), F32))
        out_specs.append(pl.BlockSpec((1, ts, da), lambda bi, i: (bi, i, 0)))
    return pl.pallas_call(
        functools.partial(_even_mid_body, ts=ts, da=da, gl=gl, want_v=want_v),
        grid=(b, s // ts),
        in_specs=[pl.BlockSpec((1, ts, n4), lambda bi, i: (bi, i, 0)),
                  pl.BlockSpec((1, HALO_A, 2 * da), lambda bi, i: (bi, jnp.maximum(i * hb - 1, 0), 0)),
                  pl.BlockSpec((1, HALO_A, da), lambda bi, i: (bi, 0, 0)),
                  pl.BlockSpec((HALO_A, da), row),
                  pl.BlockSpec((1, da), row), pl.BlockSpec((1, da), row), pl.BlockSpec((1, da), row),
                  pl.BlockSpec((1, da), row), pl.BlockSpec((1, da), row),
                  pl.BlockSpec((G_B, gl, gl), lambda bi, i: (0, 0, 0)),
                  pl.BlockSpec((gl, G_B), row)],
        out_specs=out_specs,
        out_shape=out_shape,
        scratch_shapes=[pltpu.VMEM((HALO_A + ts, da), F32), pltpu.VMEM((ts, da), F32)],
        compiler_params=_params("parallel", "arbitrary"),
        name="even_mid",
    )(z3, z3, hist, conv_w, conv_b, lag, lab, lvg, lvb, w_sp, b_sp_t)


def _proj_residual_body(*refs, n_in):
    x_refs, w_refs = refs[:n_in], refs[n_in:2 * n_in]
    r_ref, o_ref = refs[2 * n_in], refs[2 * n_in + 1]
    acc = r_ref[...]
    for x_ref, w_ref in zip(x_refs, w_refs):
        acc = acc + jnp.dot(x_ref[...], w_ref[...], preferred_element_type=F32)
    o_ref[...] = acc


def _proj_residual(xs, ws, res):
    t, n = res.shape
    tm = _tile(t, TM_MATMUL)
    tn = _tile(n, TN_MATMUL)
    in_specs = [pl.BlockSpec((tm, x.shape[1]), lambda i, j: (i, 0)) for x in xs]
    in_specs += [pl.BlockSpec((w.shape[0], tn), lambda i, j: (0, j)) for w in ws]
    in_specs.append(pl.BlockSpec((tm, tn), lambda i, j: (i, j)))
    return pl.pallas_call(
        functools.partial(_proj_residual_body, n_in=len(xs)),
        grid=(t // tm, n // tn),
        in_specs=in_specs,
        out_specs=pl.BlockSpec((tm, tn), lambda i, j: (i, j)),
        out_shape=jax.ShapeDtypeStruct((t, n), F32),
        compiler_params=_params("parallel", "arbitrary"),
        name="proj_residual",
    )(*xs, *ws, res)


def _ffn_body(x_ref, g_ref, wu_ref, wd_ref, gf_ref, o_ref, xn_ref, *, final_norm):
    j = pl.program_id(1)

    @pl.when(j == 0)
    def _():
        x = x_ref[...]
        xn_ref[...] = _rms(x, g_ref[...]).astype(BF16)
        o_ref[...] = x

    a = jnp.maximum(jnp.dot(xn_ref[...], wu_ref[...], preferred_element_type=F32), 0.0)
    o_ref[...] += jnp.dot((a * a).astype(BF16), wd_ref[...], preferred_element_type=F32)

    if final_norm:
        @pl.when(j == pl.num_programs(1) - 1)
        def _():
            o_ref[...] = _rms(o_ref[...], gf_ref[...])


def _cast_bf16_body(w_ref, o_ref):
    o_ref[...] = w_ref[...].astype(BF16)


def _cast_bf16(w):
    nl, r, c = w.shape
    tr, tc = _tile(r, CAST_BLOCK[0]), _tile(c, CAST_BLOCK[1])
    spec = pl.BlockSpec((1, tr, tc), lambda l, i, j: (l, i, j))
    return pl.pallas_call(
        _cast_bf16_body,
        grid=(nl, r // tr, c // tc),
        in_specs=[spec],
        out_specs=spec,
        out_shape=jax.ShapeDtypeStruct(w.shape, BF16),
        compiler_params=_params("parallel", "parallel", "parallel"),
        name="cast_bf16",
    )(w)


def _ffn(x, g, wu, wd, layer, gf, final_norm):
    t, d = x.shape
    f = wu.shape[2]
    tm = _tile(t, TM_MATMUL)
    tf = _tile(f, TF_FFN)
    return pl.pallas_call(
        functools.partial(_ffn_body, final_norm=final_norm),
        grid=(t // tm, f // tf),
        in_specs=[pl.BlockSpec((tm, d), lambda i, j: (i, 0)),
                  pl.BlockSpec((1, d), lambda i, j: (0, 0)),
                  pl.BlockSpec((None, d, tf), lambda i, j: (layer, 0, j)),
                  pl.BlockSpec((None, tf, d), lambda i, j: (layer, j, 0)),
                  pl.BlockSpec((1, d), lambda i, j: (0, 0))],
        out_specs=pl.BlockSpec((tm, d), lambda i, j: (i, 0)),
        out_shape=jax.ShapeDtypeStruct((t, d), F32),
        scratch_shapes=[pltpu.VMEM((tm, d), BF16)],
        compiler_params=_params("parallel", "arbitrary"),
        name="ffn",
    )(x, g.reshape(1, d), wu, wd, gf.reshape(1, d))


def _rope_fold(y, cs):
    t = y * cs
    return t + pltpu.roll(t, ROPE_DIM, axis=1)


def _odd_mid_body(zc_ref, hc_ref, hx_ref, hist_ref, cw_ref, qg_ref, kvg_ref, cs_ref,
                  cout_ref, qn_ref, ckv_ref, kr_ref, tail_ref, buf_ref, *, ts, dc, rq, rkv):
    i = pl.program_id(1)
    gx = zc_ref[0, :, dc:2 * dc] * zc_ref[0, :, 2 * dc:3 * dc]
    buf_ref[HALO_C:HALO_C + ts, :] = gx

    @pl.when(i == 0)
    def _():
        buf_ref[0:HALO_C, :] = hist_ref[0]

    @pl.when(i > 0)
    def _():
        buf_ref[0:HALO_C, :] = hc_ref[0] * hx_ref[0]

    tail_ref[0] = buf_ref[ts:ts + HALO_C, :]
    conv = cw_ref[CONV_C - 1:CONV_C, :] * gx
    for j in range(CONV_C - 1):
        off = HALO_C - (CONV_C - 1) + j
        conv = conv + cw_ref[j:j + 1, :] * buf_ref[off:off + ts, :]
    cout_ref[0] = (zc_ref[0, :, 0:dc] * conv).astype(BF16)

    o = 3 * dc
    qn_ref[0] = _rms(zc_ref[0, :, o:o + rq], qg_ref[...]).astype(BF16)
    ckv_ref[0] = _rms(zc_ref[0, :, o + rq:o + rq + rkv], kvg_ref[...])
    kr = _rope_fold(zc_ref[0, :, o + rq + rkv:o + rq + rkv + LANES], cs_ref[...])
    lane = lax.broadcasted_iota(jnp.int32, kr.shape, 1)
    kr_ref[0] = jnp.where(lane < ROPE_DIM, kr, 0.0)


def _odd_mid(z3, hist, conv_w, qg, kvg, cs_tab, dc, rq, rkv):
    b, s, n = z3.shape
    ts = _tile(s, TS_ODD)
    assert ts % HALO_C == 0 and n == 3 * dc + rq + rkv + LANES and dc % 1024 == 0
    hb = ts // HALO_C
    halo = lambda col: pl.BlockSpec((1, HALO_C, dc), lambda bi, i: (bi, jnp.maximum(i * hb - 1, 0), col))
    row = lambda bi, i: (0, 0)
    tok = lambda w: pl.BlockSpec((1, ts, w), lambda bi, i: (bi, i, 0))
    return pl.pallas_call(
        functools.partial(_odd_mid_body, ts=ts, dc=dc, rq=rq, rkv=rkv),
        grid=(b, s // ts),
        in_specs=[tok(n), halo(1), halo(2),
                  pl.BlockSpec((1, HALO_C, dc), lambda bi, i: (bi, 0, 0)),
                  pl.BlockSpec((CONV_C, dc), row),
                  pl.BlockSpec((1, rq), row), pl.BlockSpec((1, rkv), row),
                  pl.BlockSpec((ts, LANES), lambda bi, i: (i, 0))],
        out_specs=[tok(dc), tok(rq), tok(rkv), tok(LANES),
                   pl.BlockSpec((1, HALO_C, dc), lambda bi, i: (bi, 0, 0))],
        out_shape=[jax.ShapeDtypeStruct((b, s, dc), BF16),
                   jax.ShapeDtypeStruct((b, s, rq), BF16),
                   jax.ShapeDtypeStruct((b, s, rkv), F32),
                   jax.ShapeDtypeStruct((b, s, LANES), F32),
                   jax.ShapeDtypeStruct((b, HALO_C, dc), F32)],
        scratch_shapes=[pltpu.VMEM((HALO_C + ts, dc), F32)],
        compiler_params=_params("parallel", "arbitrary"),
        name="odd_mid",
    )(z3, z3, z3, hist, conv_w, qg, kvg, cs_tab)


def _q_up_body(w_ref, qn_ref, cs_ref, o_ref, *, scale):
    acc = lax.dot_general(w_ref[...], qn_ref[...], (((1,), (1,)), ((), ())), preferred_element_type=F32)
    cs = cs_ref[...]
    tq = o_ref.shape[-1]
    for h in range(o_ref.shape[1]):
        lo = h * HEAD_PAD
        nope = (acc[lo:lo + NOPE_DIM] * scale).astype(BF16)
        t = acc[lo + NOPE_DIM:lo + HEAD_PAD] * cs
        rope = ((t[:ROPE_DIM] + t[ROPE_DIM:]) * scale).astype(BF16)
        for qi in range(o_ref.shape[2]):
            cols = slice(qi * tq, (qi + 1) * tq)
            o_ref[0, h, qi, 0:NOPE_DIM, :] = nope[:, cols]
            o_ref[0, h, qi, NOPE_DIM:NOPE_DIM + ROPE_DIM, :] = rope[:, cols]
            o_ref[0, h, qi, NOPE_DIM + ROPE_DIM:HEAD_PAD, :] = jnp.zeros((ROPE_DIM, tq), BF16)


def _q_up(qn, w_t, cs_t, b, s, tq, scale):
    r = qn.shape[1]
    n = w_t.shape[0]
    tm = _tile(s, TM_MATMUL)
    tn = _tile(n, TN_MATMUL)
    per = s // tm
    hpt, qpt = tn // HEAD_PAD, tm // tq
    return pl.pallas_call(
        functools.partial(_q_up_body, scale=scale),
        grid=(b * per, n // tn),
        in_specs=[pl.BlockSpec((tn, r), lambda i, j: (j, 0)),
                  pl.BlockSpec((tm, r), lambda i, j: (i, 0)),
                  pl.BlockSpec((LANES, tm), lambda i, j: (0, i % per))],
        out_specs=pl.BlockSpec((1, hpt, qpt, HEAD_PAD, tq), lambda i, j: (i // per, j, i % per, 0, 0)),
        out_shape=jax.ShapeDtypeStruct((b, N_HEADS, s // tq, HEAD_PAD, tq), BF16),
        compiler_params=_params("parallel", "arbitrary"),
        name="q_up",
    )(w_t, qn, cs_t)


def _kv_up_body(c_ref, kr_ref, wk_ref, wvt_ref, k_ref, vt_ref):
    c = c_ref[...].astype(BF16)
    k = jnp.dot(c, wk_ref[...], preferred_element_type=F32)
    vt = lax.dot_general(wvt_ref[...], c, (((1,), (1,)), ((), ())), preferred_element_type=F32)
    kr = kr_ref[...].astype(BF16)
    tk = vt_ref.shape[-1]
    for h in range(k.shape[1] // NOPE_DIM):
        k_ref[:, h * HEAD_PAD:h * HEAD_PAD + NOPE_DIM] = k[:, h * NOPE_DIM:(h + 1) * NOPE_DIM].astype(BF16)
        k_ref[:, h * HEAD_PAD + NOPE_DIM:(h + 1) * HEAD_PAD] = kr
        for t in range(vt_ref.shape[2]):
            vt_ref[0, h, t, 0:V_DIM, :] = vt[h * V_DIM:(h + 1) * V_DIM, t * tk:(t + 1) * tk].astype(BF16)
            vt_ref[0, h, t, V_DIM:V_ROWS, :] = jnp.ones((V_ROWS - V_DIM, tk), BF16)


def _kv_up(c, kr, wk, wv_t, b, skv, tk):
    t, r = c.shape
    nkt = skv // tk
    tm = _tile(skv, KV_ROWS_STEP)
    kts = tm // tk
    per = skv // tm
    hpt = 4
    return pl.pallas_call(
        _kv_up_body,
        grid=(t // tm, N_HEADS // hpt),
        in_specs=[pl.BlockSpec((tm, r), lambda i, j: (i, 0)),
                  pl.BlockSpec((tm, LANES), lambda i, j: (i, 0)),
                  pl.BlockSpec((r, hpt * NOPE_DIM), lambda i, j: (0, j)),
                  pl.BlockSpec((hpt * V_DIM, r), lambda i, j: (j, 0))],
        out_specs=[pl.BlockSpec((tm, hpt * HEAD_PAD), lambda i, j: (i, j)),
                   pl.BlockSpec((1, hpt, kts, V_ROWS, tk), lambda i, j: (i // per, j, i % per, 0, 0))],
        out_shape=[jax.ShapeDtypeStruct((t, N_HEADS * HEAD_PAD), BF16),
                   jax.ShapeDtypeStruct((b, N_HEADS, nkt, V_ROWS, tk), BF16)],
        compiler_params=_params("parallel", "arbitrary"),
        name="kv_up",
    )(c, kr, wk, wv_t)


def _colmax(s):
    while s.shape[0] > 8 and s.shape[0] % 16 == 0:
        half = s.shape[0] // 2
        s = jnp.maximum(s[:half], s[half:])
    return jnp.max(s, axis=0, keepdims=True)


def _attention_body(qt_ref, k_ref, vt_ref, o_ref, s_ref, p_ref, acc_ref, *, tq, tk, nkt, past, t_valid):
    def k_tile(kt):
        rows = pl.ds(pl.multiple_of(jnp.minimum(kt, nkt - 1) * tk, tk), tk)
        return k_ref[0, rows, :]

    def half_step(kt, carry, slot, qt, q0, masked):
        m, alpha = carry
        other = 1 - slot
        pv = jnp.dot(vt_ref[0, 0, jnp.clip(kt - 1, 0, nkt - 1)], p_ref[other], preferred_element_type=F32)
        s_next = jnp.dot(k_tile(kt + 1), qt, preferred_element_type=F32)
        s = s_ref[slot]
        if masked:
            k_pos = kt * tk + lax.broadcasted_iota(jnp.int32, (tk, 1), 0)
            q_pos = q0 + lax.broadcasted_iota(jnp.int32, (1, tq), 1)
            k_chunk = jnp.where(k_pos < t_valid, k_pos >> CHUNK_SHIFT, jnp.iinfo(jnp.int32).max)
            s = jnp.where(k_chunk <= q_pos >> CHUNK_SHIFT, s, MASK_VALUE)
        m_new = jnp.maximum(m, _colmax(s))
        p_ref[slot] = jnp.exp2(s - m_new).astype(BF16)
        acc_ref[...] = alpha * acc_ref[...] + pv
        s_ref[other] = s_next
        return m_new, jnp.exp2(m - m_new)

    def group(i, carry, first, size, **kw):
        for g in range(size):
            carry = half_step(first + size * i + g, carry, g % 2, **kw)
        return carry

    for qi in range(qt_ref.shape[2]):
        qt = qt_ref[0, 0, qi]
        q0 = past + qi * tq
        n_full = min(((q0 >> CHUNK_SHIFT) + 1) * CHUNK, t_valid) // tk
        n_all = -(-min((((q0 + tq - 1) >> CHUNK_SHIFT) + 1) * CHUNK, t_valid) // tk)
        s_ref[0] = jnp.dot(k_tile(0), qt, preferred_element_type=F32)
        p_ref[1] = jnp.zeros((tk, tq), BF16)
        acc_ref[...] = jnp.zeros((V_ROWS, tq), F32)
        carry = (jnp.full((1, tq), MASK_VALUE, F32), jnp.ones((1, tq), F32))
        groups_full = n_full // ATT_GROUP
        carry = lax.fori_loop(0, groups_full, functools.partial(
            group, first=0, size=ATT_GROUP, qt=qt, q0=q0, masked=False), carry)
        first = groups_full * ATT_GROUP
        pairs = (n_all - first + 1) // 2
        for i in range(pairs):
            carry = group(i, carry, first, 2, qt=qt, q0=q0, masked=True)
        last = min(first + 2 * pairs - 1, nkt - 1)
        acc = carry[1] * acc_ref[...] + jnp.dot(vt_ref[0, 0, last], p_ref[1], preferred_element_type=F32)
        o_ref[0, qi * tq:(qi + 1) * tq, :] = jnp.transpose(acc[:V_DIM] / acc[V_DIM:V_DIM + 1]).astype(BF16)


def _attention(qt, k3, vt, past, t_valid):
    b, _, nq, _, tq = qt.shape
    skv = k3.shape[1]
    nkt, tk = vt.shape[2], vt.shape[4]
    return pl.pallas_call(
        functools.partial(_attention_body, tq=tq, tk=tk, nkt=nkt, past=past, t_valid=t_valid),
        scratch_shapes=[pltpu.VMEM((2, tk, tq), F32), pltpu.VMEM((2, tk, tq), BF16), pltpu.VMEM((V_ROWS, tq), F32)],
        grid=(b, N_HEADS),
        in_specs=[pl.BlockSpec((1, 1, nq, HEAD_PAD, tq), lambda bi, h: (bi, h, 0, 0, 0)),
                  pl.BlockSpec((1, skv, HEAD_PAD), lambda bi, h: (bi, 0, h)),
                  pl.BlockSpec((1, 1, nkt, V_ROWS, tk), lambda bi, h: (bi, h, 0, 0, 0))],
        out_specs=pl.BlockSpec((1, nq * tq, V_DIM), lambda bi, h: (bi, 0, h)),
        out_shape=jax.ShapeDtypeStruct((b, nq * tq, N_HEADS * V_DIM), BF16),
        compiler_params=_params("parallel", "parallel"),
        name="attention",
    )(qt, k3, vt)


def _latent_attention_body(qn_ref, wq_ref, wuk_ref, cs_ref, klat_ref, wuv_ref, o_ref, q_ref, acc_ref,
                           *, ss, tk, n_tiles, past, t_valid, scale, rkv):
    qn = qn_ref[0]
    cs = cs_ref[...]
    lane = lax.broadcasted_iota(jnp.int32, (ss, LANES), 1)
    for h in range(N_HEADS):
        qh = jnp.dot(qn, wq_ref[:, h * HEAD_PAD:(h + 1) * HEAD_PAD], preferred_element_type=F32)
        q_abs = lax.dot_general(qh[:, :NOPE_DIM].astype(BF16), wuk_ref[:, h * NOPE_DIM:(h + 1) * NOPE_DIM],
                                (((1,), (1,)), ((), ())), preferred_element_type=F32)
        rope = jnp.where(lane < ROPE_DIM, _rope_fold(qh[:, NOPE_DIM:], cs), 0.0)
        q_ref[h * ss:(h + 1) * ss, 0:rkv] = (q_abs * scale).astype(BF16)
        q_ref[h * ss:(h + 1) * ss, rkv:rkv + LANES] = (rope * scale).astype(BF16)

    q = q_ref[...]
    cols = N_HEADS * ss
    q_chunk = (past + lax.broadcasted_iota(jnp.int32, (1, cols), 1) % ss) >> CHUNK_SHIFT
    acc_ref[...] = jnp.zeros((rkv, cols), F32)

    def step(kt, carry):
        m, l = carry
        tile = klat_ref[0, pl.ds(pl.multiple_of(kt * tk, tk), tk), :]
        s = lax.dot_general(tile, q, (((1,), (1,)), ((), ())), preferred_element_type=F32)
        k_pos = kt * tk + lax.broadcasted_iota(jnp.int32, (tk, 1), 0)
        k_chunk = jnp.where(k_pos < t_valid, k_pos >> CHUNK_SHIFT, jnp.iinfo(jnp.int32).max)
        s = jnp.where(k_chunk <= q_chunk, s, MASK_VALUE)
        m_new = jnp.maximum(m, _colmax(s))
        alpha = jnp.exp2(m - m_new)
        p = jnp.exp2(s - m_new)
        pv = lax.dot_general(tile[:, :rkv], p.astype(BF16), (((0,), (0,)), ((), ())), preferred_element_type=F32)
        acc_ref[...] = alpha * acc_ref[...] + pv
        return m_new, alpha * l + jnp.sum(p, axis=0, keepdims=True)

    init = (jnp.full((1, cols), MASK_VALUE, F32), jnp.zeros((1, cols), F32))
    _, l = lax.fori_loop(0, n_tiles, step, init)
    o_lat = jnp.transpose(acc_ref[...] / l).astype(BF16)
    for h in range(N_HEADS):
        o_ref[0, :, h * V_DIM:(h + 1) * V_DIM] = jnp.dot(
            o_lat[h * ss:(h + 1) * ss], wuv_ref[:, h * V_DIM:(h + 1) * V_DIM],
            preferred_element_type=F32).astype(BF16)


def _latent_attention(qn, wq, wuk, cs_tab, klat, wuv, past, t_valid, scale):
    b, ss, rq = qn.shape
    skv, feat = klat.shape[1], klat.shape[2]
    rkv = feat - LANES
    tk = min(TK_ATT, skv)
    see_any = min((((past + ss - 1) >> CHUNK_SHIFT) + 1) * CHUNK, t_valid)
    n_tiles = -(-see_any // tk)
    assert skv % tk == 0 and n_tiles * tk <= skv
    whole = lambda a: pl.BlockSpec(a.shape, lambda bi: (0,) * a.ndim)
    return pl.pallas_call(
        functools.partial(_latent_attention_body, ss=ss, tk=tk, n_tiles=n_tiles, past=past, t_valid=t_valid,
                          scale=scale, rkv=rkv),
        grid=(b,),
        in_specs=[pl.BlockSpec((1, ss, rq), lambda bi: (bi, 0, 0)), whole(wq), whole(wuk), whole(cs_tab),
                  pl.BlockSpec((1, skv, feat), lambda bi: (bi, 0, 0)), whole(wuv)],
        out_specs=pl.BlockSpec((1, ss, N_HEADS * V_DIM), lambda bi: (bi, 0, 0)),
        out_shape=jax.ShapeDtypeStruct((b, ss, N_HEADS * V_DIM), BF16),
        scratch_shapes=[pltpu.VMEM((N_HEADS * ss, feat), BF16), pltpu.VMEM((rkv, N_HEADS * ss), F32)],
        compiler_params=_params("parallel"),
        name="latent_attention",
    )(qn, wq, wuk, cs_tab, klat, wuv)


def _rope_table(past, s):
    half = ROPE_DIM // 2
    inv = ROPE_THETA ** (-jnp.arange(half, dtype=F32) / half)
    ang = (past + jnp.arange(s, dtype=jnp.int32)).astype(F32)[:, None] * inv[None, :]
    cos, sin = jnp.cos(ang), jnp.sin(ang)
    return jnp.concatenate([cos, cos, sin, sin], axis=-1)


def _rot_cols(w):
    half = ROPE_DIM // 2
    return jnp.concatenate([-w[..., half:], w[..., :half]], axis=-1)


def _pad_rows_front(x, rows):
    return jnp.pad(x, ((0, 0), (rows - x.shape[1], 0), (0, 0)))


def _even_layer(h, b, s, hist, wts, want_v):
    (g_mix, w_in, conv_w, conv_b, lag, lab, lvg, lvb, w_sp, b_sp_t, w_out) = wts
    z = _norm_matmul(h, g_mix, w_in, _tile(w_in.shape[1], TN_MATMUL))
    outs = _even_mid(z.reshape(b, s, -1), hist, conv_w, conv_b, lag, lab, lvg, lvb, w_sp, b_sp_t, want_v)
    cat, tail = outs[0], outs[1]
    h = _proj_residual([cat.reshape(b * s, -1)], [w_out], h)
    return h, tail[:, HALO_A - (CONV_A - 1):], (outs[2] if want_v else None)


def _odd_layer(h, b, s, hist, lat_prev, kr_prev, wts, cs_tab):
    (g_mix, w_in, conv_w, qg, wq, kvg, w_k, w_v, w_out_c, w_out_a) = wts
    dc = conv_w.shape[1]
    rq, rkv = qg.shape[1], kvg.shape[1]
    z = _norm_matmul(h, g_mix, w_in, w_in.shape[1] // 3)
    cout, qn, ckv, kr, tail = _odd_mid(z.reshape(b, s, -1), hist, conv_w, qg, kvg, cs_tab, dc, rq, rkv)
    scale = math.log2(math.e) / math.sqrt(NOPE_DIM + ROPE_DIM)
    if lat_prev is None:
        assert s % LANES == 0
        tk = min(TK_ATT, s)
        qt = _q_up(qn.reshape(b * s, rq), wq.T, cs_tab.T, b, s, min(TQ_ATT, s), scale)
        k, vt = _kv_up(ckv.reshape(b * s, rkv), kr.reshape(b * s, LANES), w_k, w_v.T, b, s, tk)
        attn = _attention(qt, k.reshape(b, s, -1), vt, 0, s)
    else:
        past = lat_prev.shape[1]
        t_valid = past + s
        tk = min(TK_ATT, t_valid)
        skv = -(-t_valid // tk) * tk
        kr_prev = jnp.pad(kr_prev, ((0, 0), (0, 0), (0, LANES - ROPE_DIM)))
        klat = jnp.concatenate([jnp.concatenate([lat_prev, ckv], axis=1),
                                jnp.concatenate([kr_prev, kr], axis=1)], axis=-1).astype(BF16)
        klat = jnp.pad(klat, ((0, 0), (0, skv - t_valid), (0, 0)))
        attn = _latent_attention(qn, wq, w_k, cs_tab, klat, w_v, past, t_valid, scale)
    h = _proj_residual([cout.reshape(b * s, dc), attn.reshape(b * s, -1)], [w_out_c, w_out_a], h)
    return h, tail[:, HALO_C - (CONV_C - 1):], ckv, kr[..., :ROPE_DIM]


def kernel(x_prompt, x_sample, state_conv_a, state_conv_c, cache_kv_latent, cache_k_rope, norm_mix, norm_ffn, norm_final, w_in_even, conv_a_w, conv_a_b, ln_a_g, ln_a_b, ln_v_g, ln_v_b, w_spatial, b_spatial, w_out_even, w_in_odd, conv_c_w, q_norm_g, w_uq, kv_norm_g, w_ukv, w_out_odd, w_ffn_up, w_ffn_down):
    bp, sp, d = x_prompt.shape
    bs, ss, _ = x_sample.shape
    depth = norm_mix.shape[0]
    da = conv_a_w.shape[-1]
    dc = conv_c_w.shape[-1]
    rq, rkv = q_norm_g.shape[-1], kv_norm_g.shape[-1]
    past = cache_kv_latent.shape[2]
    assert sp % GMLP_CHUNK == 0 and ss <= GMLP_CHUNK and ss >= HALO_A and past % GMLP_CHUNK == 0

    hp = x_prompt.reshape(bp * sp, d)
    hs = x_sample.reshape(bs * ss, d)
    cs_p = _rope_table(0, sp)
    cs_s = _rope_table(past, ss)
    wu_all, wd_all = _cast_bf16(w_ffn_up), _cast_bf16(w_ffn_down)
    w_in_even_bf, w_out_even_bf, w_out_odd_bf = _cast_bf16(w_in_even), _cast_bf16(w_out_even), _cast_bf16(w_out_odd)

    ca_p, ca_s, gv_s, cc_p, cc_s, lat_p, kr_p, lat_s, kr_s = [], [], [], [], [], [], [], [], []
    for l in range(depth):
        p = l // 2
        if l % 2 == 0:
            row = lambda a: a[p].reshape(1, -1)
            gl_p, gl_s = min(sp, GMLP_CHUNK), min(ss, GMLP_CHUNK)
            common = (norm_mix[l], w_in_even_bf[p],
                      jnp.pad(conv_a_w[p], ((0, HALO_A - CONV_A), (0, 0))), row(conv_a_b),
                      row(ln_a_g), row(ln_a_b), row(ln_v_g), row(ln_v_b))
            w_out = w_out_even_bf[p]
            wts_p = common + (w_spatial[p][:, :gl_p, :gl_p], b_spatial[p][:, :gl_p].T, w_out)
            wts_s = common + (w_spatial[p][:, :gl_s, :gl_s], b_spatial[p][:, :gl_s].T, w_out)
            hp, tail_p, _ = _even_layer(hp, bp, sp, jnp.zeros((bp, HALO_A, da), F32), wts_p, False)
            hs, tail_s, v_s = _even_layer(hs, bs, ss, _pad_rows_front(state_conv_a[p], HALO_A), wts_s, True)
            ca_p.append(tail_p)
            ca_s.append(tail_s)
            gv_s.append(v_s)
        else:
            w_in = w_in_odd[p]
            n_main = 3 * dc + rq + rkv
            w_in = jnp.concatenate([w_in, _rot_cols(w_in[:, n_main:])], axis=1).astype(BF16)
            wq = w_uq[p].reshape(rq, N_HEADS, NOPE_DIM + ROPE_DIM)
            wq = jnp.concatenate([wq, _rot_cols(wq[..., NOPE_DIM:])], axis=-1).reshape(rq, N_HEADS * HEAD_PAD)
            wq = wq.astype(BF16)
            wkv = w_ukv[p].reshape(rkv, N_HEADS, NOPE_DIM + V_DIM)
            wk = wkv[..., :NOPE_DIM].reshape(rkv, N_HEADS * NOPE_DIM).astype(BF16)
            wv = wkv[..., NOPE_DIM:].reshape(rkv, N_HEADS * V_DIM).astype(BF16)
            w_out = w_out_odd_bf[p]
            wts = (norm_mix[l], w_in, conv_c_w[p], q_norm_g[p].reshape(1, rq), wq,
                   kv_norm_g[p].reshape(1, rkv), wk, wv, w_out[:dc], w_out[dc:])
            hp, tail_p, c_p, r_p = _odd_layer(hp, bp, sp, jnp.zeros((bp, HALO_C, dc), F32), None, None,
                                              wts, cs_p)
            hs, tail_s, c_s, r_s = _odd_layer(hs, bs, ss, _pad_rows_front(state_conv_c[p], HALO_C),
                                              cache_kv_latent[p], cache_k_rope[p], wts, cs_s)
            cc_p.append(tail_p)
            cc_s.append(tail_s)
            lat_p.append(c_p)
            kr_p.append(r_p)
            lat_s.append(c_s)
            kr_s.append(r_s)
        last = l == depth - 1
        hp = _ffn(hp, norm_ffn[l], wu_all, wd_all, l, norm_final, last)
        hs = _ffn(hs, norm_ffn[l], wu_all, wd_all, l, norm_final, last)

    return (hp.reshape(bp, sp, d), hs.reshape(bs, ss, d),
            jnp.stack(ca_p), jnp.stack(ca_s), jnp.stack(gv_s), jnp.stack(cc_p), jnp.stack(cc_s),
            jnp.stack(lat_p), jnp.stack(kr_p), jnp.stack(lat_s), jnp.stack(kr_s))
```

```python
import functools
import math

import jax
import jax.numpy as jnp
from jax import lax
from jax.experimental import pallas as pl
from jax.experimental.pallas import tpu as pltpu

F32 = jnp.float32
BF16 = jnp.bfloat16

CHUNK = 64
CHUNK_SHIFT = 6
CONV_A = 31
CONV_C = 3
G_B = 8
GMLP_CHUNK = 128
N_HEADS = 16
NOPE_DIM = 128
ROPE_DIM = 64
V_DIM = 128
V_ROWS = V_DIM + 16
ROPE_THETA = 10000.0
EPS = 1e-6

LANES = 128
SUBLANES = 8
HEAD_PAD = 2 * LANES
HALO_A = 32
HALO_C = 8
MASK_VALUE = -1e30
VMEM_LIMIT = 56 * 1024 * 1024

TM_MATMUL = 1024
TN_MATMUL = 1024
TF_FFN = 512
TS_EVEN = 128
TS_ODD = 512
TQ_ATT = 512
TK_ATT = 512
CAST_BLOCK = (1024, 2048)
ATT_GROUP = 4
KV_ROWS_STEP = 1024


def _params(*sem):
    return pltpu.CompilerParams(dimension_semantics=sem, vmem_limit_bytes=VMEM_LIMIT)


def _rms(x, g):
    return x * lax.rsqrt(jnp.mean(x * x, axis=-1, keepdims=True) + EPS) * g


def _layer_norm(x, g, b):
    xc = x - jnp.mean(x, axis=-1, keepdims=True)
    return xc * lax.rsqrt(jnp.mean(xc * xc, axis=-1, keepdims=True) + EPS) * g + b


def _sigmoid(x):
    return 0.5 * jnp.tanh(0.5 * x) + 0.5


def _gelu(x):
    return 0.5 * x * (1.0 + lax.erf(x * math.sqrt(0.5)))


def _tile(n, t):
    t = min(n, t)
    assert n % t == 0, (n, t)
    return t


def _norm_matmul_body(x_ref, g_ref, w_ref, o_ref, xn_ref):
    @pl.when(pl.program_id(1) == 0)
    def _():
        xn_ref[...] = _rms(x_ref[...], g_ref[...]).astype(BF16)

    o_ref[...] = jnp.dot(xn_ref[...], w_ref[...], preferred_element_type=F32)


def _norm_matmul(x, g, w, tn):
    t, d = x.shape
    n = w.shape[1]
    tm = _tile(t, TM_MATMUL)
    return pl.pallas_call(
        _norm_matmul_body,
        grid=(t // tm, n // tn),
        in_specs=[pl.BlockSpec((tm, d), lambda i, j: (i, 0)),
                  pl.BlockSpec((1, d), lambda i, j: (0, 0)),
                  pl.BlockSpec((d, tn), lambda i, j: (0, j))],
        out_specs=pl.BlockSpec((tm, tn), lambda i, j: (i, j)),
        out_shape=jax.ShapeDtypeStruct((t, n), F32),
        scratch_shapes=[pltpu.VMEM((tm, d), BF16)],
        compiler_params=_params("parallel", "arbitrary"),
        name="norm_matmul",
    )(x, g.reshape(1, d), w)


def _even_mid_body(*refs, ts, da, gl, want_v):
    (zc_ref, zh_ref, hist_ref, cw_ref, cb_ref, lag_ref, lab_ref, lvg_ref, lvb_ref,
     wsp_ref, bsp_ref, cat_ref, tail_ref) = refs[:13]
    v_ref = refs[13] if want_v else None
    buf_ref, conv_ref = refs[-2:]
    i = pl.program_id(1)

    buf_ref[HALO_A:HALO_A + ts, :] = zc_ref[0, :, 0:da] * _sigmoid(zc_ref[0, :, da:2 * da])

    @pl.when(i == 0)
    def _():
        buf_ref[0:HALO_A, :] = hist_ref[0]

    @pl.when(i > 0)
    def _():
        buf_ref[0:HALO_A, :] = zh_ref[0, :, 0:da] * _sigmoid(zh_ref[0, :, da:2 * da])

    tail_ref[0] = buf_ref[ts:ts + HALO_A, :]

    rc = min(ts, 128)
    base = HALO_A - (CONV_A - 1)
    win = rc + HALO_A
    for c in range(da // LANES):
        cs = slice(c * LANES, (c + 1) * LANES)
        for r0 in range(0, ts, rc):
            window = buf_ref[r0:r0 + win, cs]
            acc = jnp.broadcast_to(cb_ref[:, cs], (rc, LANES))
            for sh in range(SUBLANES):
                rolled = window if sh == 0 else pltpu.roll(window, win - sh, axis=0)
                for off in range(sh, HALO_A + 1, SUBLANES):
                    j = off - base
                    if 0 <= j < CONV_A:
                        acc = acc + cw_ref[j:j + 1, cs] * rolled[off - sh:off - sh + rc]
            conv_ref[r0:r0 + rc, cs] = acc

    y = _layer_norm(conv_ref[...], lag_ref[...], lab_ref[...])
    cat_ref[0, :, 0:da] = (y * _sigmoid(y)).astype(BF16)

    u = _gelu(zc_ref[0, :, 2 * da:3 * da])
    v = _layer_norm(_gelu(zc_ref[0, :, 3 * da:4 * da]), lvg_ref[...], lvb_ref[...])
    if want_v:
        v_ref[0] = v
    vb = v.astype(BF16)
    dh = da // G_B
    tril = lax.broadcasted_iota(jnp.int32, (gl, gl), 0) >= lax.broadcasted_iota(jnp.int32, (gl, gl), 1)
    for g in range(G_B):
        wg = jnp.where(tril, wsp_ref[g], 0.0).astype(BF16)
        gs = slice(g * dh, (g + 1) * dh)
        for k in range(ts // gl):
            rs = slice(k * gl, (k + 1) * gl)
            mixed = jnp.dot(wg, vb[rs, gs], preferred_element_type=F32) + bsp_ref[:, g:g + 1]
            cat_ref[0, rs, da + g * dh:da + (g + 1) * dh] = (u[rs, gs] * mixed).astype(BF16)


def _even_mid(z3, hist, conv_w, conv_b, lag, lab, lvg, lvb, w_sp, b_sp_t, want_v):
    b, s, n4 = z3.shape
    da = n4 // 4
    gl = w_sp.shape[-1]
    ts = _tile(s, TS_EVEN)
    assert ts % gl == 0 and ts % HALO_A == 0
    hb = ts // HALO_A
    row = lambda bi, i: (0, 0)
    out_shape = [jax.ShapeDtypeStruct((b, s, 2 * da), BF16), jax.ShapeDtypeStruct((b, HALO_A, da), F32)]
    out_specs = [pl.BlockSpec((1, ts, 2 * da), lambda bi, i: (bi, i, 0)),
                 pl.BlockSpec((1, HALO_A, da), lambda bi, i: (bi, 0, 0))]
    if want_v:
        out_shape.append(jax.ShapeDtypeStruct((b, s, da), F32))
        out_specs.append(pl.BlockSpec((1, ts, da), lambda bi, i: (bi, i, 0)))
    return pl.pallas_call(
        functools.partial(_even_mid_body, ts=ts, da=da, gl=gl, want_v=want_v),
        grid=(b, s // ts),
        in_specs=[pl.BlockSpec((1, ts, n4), lambda bi, i: (bi, i, 0)),
                  pl.BlockSpec((1, HALO_A, 2 * da), lambda bi, i: (bi, jnp.maximum(i * hb - 1, 0), 0)),
                  pl.BlockSpec((1, HALO_A, da), lambda bi, i: (bi, 0, 0)),
                  pl.BlockSpec((HALO_A, da), row),
                  pl.BlockSpec((1, da), row), pl.BlockSpec((1, da), row), pl.BlockSpec((1, da), row),
                  pl.BlockSpec((1, da), row), pl.BlockSpec((1, da), row),
                  pl.BlockSpec((G_B, gl, gl), lambda bi, i: (0, 0, 0)),
                  pl.BlockSpec((gl, G_B), row)],
        out_specs=out_specs,
        out_shape=out_shape,
        scratch_shapes=[pltpu.VMEM((HALO_A + ts, da), F32), pltpu.VMEM((ts, da), F32)],
        compiler_params=_params("parallel", "arbitrary"),
        name="even_mid",
    )(z3, z3, hist, conv_w, conv_b, lag, lab, lvg, lvb, w_sp, b_sp_t)


def _proj_residual_body(*refs, n_in):
    x_refs, w_refs = refs[:n_in], refs[n_in:2 * n_in]
    r_ref, o_ref = refs[2 * n_in], refs[2 * n_in + 1]
    acc = r_ref[...]
    for x_ref, w_ref in zip(x_refs, w_refs):
        acc = acc + jnp.dot(x_ref[...], w_ref[...], preferred_element_type=F32)
    o_ref[...] = acc


def _proj_residual(xs, ws, res):
    t, n = res.shape
    tm = _tile(t, TM_MATMUL)
    tn = _tile(n, TN_MATMUL)
    in_specs = [pl.BlockSpec((tm, x.shape[1]), lambda i, j: (i, 0)) for x in xs]
    in_specs += [pl.BlockSpec((w.shape[0], tn), lambda i, j: (0, j)) for w in ws]
    in_specs.append(pl.BlockSpec((tm, tn), lambda i, j: (i, j)))
    return pl.pallas_call(
        functools.partial(_proj_residual_body, n_in=len(xs)),
        grid=(t // tm, n // tn),
        in_specs=in_specs,
        out_specs=pl.BlockSpec((tm, tn), lambda i, j: (i, j)),
        out_shape=jax.ShapeDtypeStruct((t, n), F32),
        compiler_params=_params("parallel", "arbitrary"),
        name="proj_residual",
    )(*xs, *ws, res)


def _ffn_body(x_ref, g_ref, wu_ref, wd_ref, gf_ref, o_ref, xn_ref, *, final_norm):
    j = pl.program_id(1)

    @pl.when(j == 0)
    def _():
        x = x_ref[...]
        xn_ref[...] = _rms(x, g_ref[...]).astype(BF16)
        o_ref[...] = x

    a = jnp.maximum(jnp.dot(xn_ref[...], wu_ref[...], preferred_element_type=F32), 0.0)
    o_ref[...] += jnp.dot((a * a).astype(BF16), wd_ref[...], preferred_element_type=F32)

    if final_norm:
        @pl.when(j == pl.num_programs(1) - 1)
        def _():
            o_ref[...] = _rms(o_ref[...], gf_ref[...])


def _cast_bf16_body(w_ref, o_ref):
    o_ref[...] = w_ref[...].astype(BF16)


def _cast_bf16(w):
    nl, r, c = w.shape
    tr, tc = _tile(r, CAST_BLOCK[0]), _tile(c, CAST_BLOCK[1])
    spec = pl.BlockSpec((1, tr, tc), lambda l, i, j: (l, i, j))
    return pl.pallas_call(
        _cast_bf16_body,
        grid=(nl, r // tr, c // tc),
        in_specs=[spec],
        out_specs=spec,
        out_shape=jax.ShapeDtypeStruct(w.shape, BF16),
        compiler_params=_params("parallel", "parallel", "parallel"),
        name="cast_bf16",
    )(w)


def _ffn(x, g, wu, wd, layer, gf, final_norm):
    t, d = x.shape
    f = wu.shape[2]
    tm = _tile(t, TM_MATMUL)
    tf = _tile(f, TF_FFN)
    return pl.pallas_call(
        functools.partial(_ffn_body, final_norm=final_norm),
        grid=(t // tm, f // tf),
        in_specs=[pl.BlockSpec((tm, d), lambda i, j: (i, 0)),
                  pl.BlockSpec((1, d), lambda i, j: (0, 0)),
                  pl.BlockSpec((None, d, tf), lambda i, j: (layer, 0, j)),
                  pl.BlockSpec((None, tf, d), lambda i, j: (layer, j, 0)),
                  pl.BlockSpec((1, d), lambda i, j: (0, 0))],
        out_specs=pl.BlockSpec((tm, d), lambda i, j: (i, 0)),
        out_shape=jax.ShapeDtypeStruct((t, d), F32),
        scratch_shapes=[pltpu.VMEM((tm, d), BF16)],
        compiler_params=_params("parallel", "arbitrary"),
        name="ffn",
    )(x, g.reshape(1, d), wu, wd, gf.reshape(1, d))


def _rope_fold(y, cs):
    t = y * cs
    return t + pltpu.roll(t, ROPE_DIM, axis=1)


def _odd_mid_body(zc_ref, hc_ref, hx_ref, hist_ref, cw_ref, qg_ref, kvg_ref, cs_ref,
                  cout_ref, qn_ref, ckv_ref, kr_ref, tail_ref, buf_ref, *, ts, dc, rq, rkv):
    i = pl.program_id(1)
    gx = zc_ref[0, :, dc:2 * dc] * zc_ref[0, :, 2 * dc:3 * dc]
    buf_ref[HALO_C:HALO_C + ts, :] = gx

    @pl.when(i == 0)
    def _():
        buf_ref[0:HALO_C, :] = hist_ref[0]

    @pl.when(i > 0)
    def _():
        buf_ref[0:HALO_C, :] = hc_ref[0] * hx_ref[0]

    tail_ref[0] = buf_ref[ts:ts + HALO_C, :]
    conv = cw_ref[CONV_C - 1:CONV_C, :] * gx
    for j in range(CONV_C - 1):
        off = HALO_C - (CONV_C - 1) + j
        conv = conv + cw_ref[j:j + 1, :] * buf_ref[off:off + ts, :]
    cout_ref[0] = (zc_ref[0, :, 0:dc] * conv).astype(BF16)

    o = 3 * dc
    qn_ref[0] = _rms(zc_ref[0, :, o:o + rq], qg_ref[...]).astype(BF16)
    ckv_ref[0] = _rms(zc_ref[0, :, o + rq:o + rq + rkv], kvg_ref[...])
    kr = _rope_fold(zc_ref[0, :, o + rq + rkv:o + rq + rkv + LANES], cs_ref[...])
    lane = lax.broadcasted_iota(jnp.int32, kr.shape, 1)
    kr_ref[0] = jnp.where(lane < ROPE_DIM, kr, 0.0)


def _odd_mid(z3, hist, conv_w, qg, kvg, cs_tab, dc, rq, rkv):
    b, s, n = z3.shape
    ts = _tile(s, TS_ODD)
    assert ts % HALO_C == 0 and n == 3 * dc + rq + rkv + LANES and dc % 1024 == 0
    hb = ts // HALO_C
    halo = lambda col: pl.BlockSpec((1, HALO_C, dc), lambda bi, i: (bi, jnp.maximum(i * hb - 1, 0), col))
    row = lambda bi, i: (0, 0)
    tok = lambda w: pl.BlockSpec((1, ts, w), lambda bi, i: (bi, i, 0))
    return pl.pallas_call(
        functools.partial(_odd_mid_body, ts=ts, dc=dc, rq=rq, rkv=rkv),
        grid=(b, s // ts),
        in_specs=[tok(n), halo(1), halo(2),
                  pl.BlockSpec((1, HALO_C, dc), lambda bi, i: (bi, 0, 0)),
                  pl.BlockSpec((CONV_C, dc), row),
                  pl.BlockSpec((1, rq), row), pl.BlockSpec((1, rkv), row),
                  pl.BlockSpec((ts, LANES), lambda bi, i: (i, 0))],
        out_specs=[tok(dc), tok(rq), tok(rkv), tok(LANES),
                   pl.BlockSpec((1, HALO_C, dc), lambda bi, i: (bi, 0, 0))],
        out_shape=[jax.ShapeDtypeStruct((b, s, dc), BF16),
                   jax.ShapeDtypeStruct((b, s, rq), BF16),
                   jax.ShapeDtypeStruct((b, s, rkv), F32),
                   jax.ShapeDtypeStruct((b, s, LANES), F32),
                   jax.ShapeDtypeStruct((b, HALO_C, dc), F32)],
        scratch_shapes=[pltpu.VMEM((HALO_C + ts, dc), F32)],
        compiler_params=_params("parallel", "arbitrary"),
        name="odd_mid",
    )(z3, z3, z3, hist, conv_w, qg, kvg, cs_tab)


def _q_up_body(w_ref, qn_ref, cs_ref, o_ref, *, scale):
    acc = lax.dot_general(w_ref[...], qn_ref[...], (((1,), (1,)), ((), ())), preferred_element_type=F32)
    cs = cs_ref[...]
    tq = o_ref.shape[-1]
    for h in range(o_ref.shape[1]):
        lo = h * HEAD_PAD
        nope = (acc[lo:lo + NOPE_DIM] * scale).astype(BF16)
        t = acc[lo + NOPE_DIM:lo + HEAD_PAD] * cs
        rope = ((t[:ROPE_DIM] + t[ROPE_DIM:]) * scale).astype(BF16)
        for qi in range(o_ref.shape[2]):
            cols = slice(qi * tq, (qi + 1) * tq)
            o_ref[0, h, qi, 0:NOPE_DIM, :] = nope[:, cols]
            o_ref[0, h, qi, NOPE_DIM:NOPE_DIM + ROPE_DIM, :] = rope[:, cols]
            o_ref[0, h, qi, NOPE_DIM + ROPE_DIM:HEAD_PAD, :] = jnp.zeros((ROPE_DIM, tq), BF16)


def _q_up(qn, w_t, cs_t, b, s, tq, scale):
    r = qn.shape[1]
    n = w_t.shape[0]
    tm = _tile(s, TM_MATMUL)
    tn = _tile(n, TN_MATMUL)
    per = s // tm
    hpt, qpt = tn // HEAD_PAD, tm // tq
    return pl.pallas_call(
        functools.partial(_q_up_body, scale=scale),
        grid=(b * per, n // tn),
        in_specs=[pl.BlockSpec((tn, r), lambda i, j: (j, 0)),
                  pl.BlockSpec((tm, r), lambda i, j: (i, 0)),
                  pl.BlockSpec((LANES, tm), lambda i, j: (0, i % per))],
        out_specs=pl.BlockSpec((1, hpt, qpt, HEAD_PAD, tq), lambda i, j: (i // per, j, i % per, 0, 0)),
        out_shape=jax.ShapeDtypeStruct((b, N_HEADS, s // tq, HEAD_PAD, tq), BF16),
        compiler_params=_params("parallel", "arbitrary"),
        name="q_up",
    )(w_t, qn, cs_t)


def _kv_up_body(c_ref, kr_ref, wk_ref, wvt_ref, k_ref, vt_ref):
    c = c_ref[...].astype(BF16)
    k = jnp.dot(c, wk_ref[...], preferred_element_type=F32)
    vt = lax.dot_general(wvt_ref[...], c, (((1,), (1,)), ((), ())), preferred_element_type=F32)
    kr = kr_ref[...].astype(BF16)
    tk = vt_ref.shape[-1]
    for h in range(k.shape[1] // NOPE_DIM):
        k_ref[:, h * HEAD_PAD:h * HEAD_PAD + NOPE_DIM] = k[:, h * NOPE_DIM:(h + 1) * NOPE_DIM].astype(BF16)
        k_ref[:, h * HEAD_PAD + NOPE_DIM:(h + 1) * HEAD_PAD] = kr
        for t in range(vt_ref.shape[2]):
            vt_ref[0, h, t, 0:V_DIM, :] = vt[h * V_DIM:(h + 1) * V_DIM, t * tk:(t + 1) * tk].astype(BF16)
            vt_ref[0, h, t, V_DIM:V_ROWS, :] = jnp.ones((V_ROWS - V_DIM, tk), BF16)


def _kv_up(c, kr, wk, wv_t, b, skv, tk):
    t, r = c.shape
    nkt = skv // tk
    tm = _tile(skv, KV_ROWS_STEP)
    kts = tm // tk
    per = skv // tm
    hpt = 4
    return pl.pallas_call(
        _kv_up_body,
        grid=(t // tm, N_HEADS // hpt),
        in_specs=[pl.BlockSpec((tm, r), lambda i, j: (i, 0)),
                  pl.BlockSpec((tm, LANES), lambda i, j: (i, 0)),
                  pl.BlockSpec((r, hpt * NOPE_DIM), lambda i, j: (0, j)),
                  pl.BlockSpec((hpt * V_DIM, r), lambda i, j: (j, 0))],
        out_specs=[pl.BlockSpec((tm, hpt * HEAD_PAD), lambda i, j: (i, j)),
                   pl.BlockSpec((1, hpt, kts, V_ROWS, tk), lambda i, j: (i // per, j, i % per, 0, 0))],
        out_shape=[jax.ShapeDtypeStruct((t, N_HEADS * HEAD_PAD), BF16),
                   jax.ShapeDtypeStruct((b, N_HEADS, nkt, V_ROWS, tk), BF16)],
        compiler_params=_params("parallel", "arbitrary"),
        name="kv_up",
    )(c, kr, wk, wv_t)


def _colmax(s):
    while s.shape[0] > 8 and s.shape[0] % 16 == 0:
        half = s.shape[0] // 2
        s = jnp.maximum(s[:half], s[half:])
    return jnp.max(s, axis=0, keepdims=True)


def _attention_body(qt_ref, k_ref, vt_ref, o_ref, s_ref, p_ref, acc_ref, *, tq, tk, nkt, past, t_valid):
    def k_tile(kt):
        return k_ref[0, kt * tk:(kt + 1) * tk, :]

    def half_step(kt, n_steps, carry, qt, q0, masked):
        m, alpha = carry
        slot, other = kt % 2, 1 - kt % 2
        if kt >= 1:
            pv = jnp.dot(vt_ref[0, 0, kt - 1], p_ref[other], preferred_element_type=F32)
        if kt + 1 < n_steps:
            s_next = jnp.dot(k_tile(kt + 1), qt, preferred_element_type=F32)
        s = s_ref[slot]
        if masked:
            k_pos = kt * tk + lax.broadcasted_iota(jnp.int32, (tk, 1), 0)
            q_pos = q0 + lax.broadcasted_iota(jnp.int32, (1, tq), 1)
            k_chunk = jnp.where(k_pos < t_valid, k_pos >> CHUNK_SHIFT, jnp.iinfo(jnp.int32).max)
            s = jnp.where(k_chunk <= q_pos >> CHUNK_SHIFT, s, MASK_VALUE)
        m_new = jnp.maximum(m, _colmax(s))
        p_ref[slot] = jnp.exp2(s - m_new).astype(BF16)
        if kt == 1:
            acc_ref[...] = pv
        elif kt > 1:
            acc_ref[...] = alpha * acc_ref[...] + pv
        if kt + 1 < n_steps:
            s_ref[other] = s_next
        return m_new, jnp.exp2(m - m_new)

    for qi in range(qt_ref.shape[2]):
        qt = qt_ref[0, 0, qi]
        q0 = past + qi * tq
        n_full = min(((q0 >> CHUNK_SHIFT) + 1) * CHUNK, t_valid) // tk
        n_all = -(-min((((q0 + tq - 1) >> CHUNK_SHIFT) + 1) * CHUNK, t_valid) // tk)
        s_ref[0] = jnp.dot(k_tile(0), qt, preferred_element_type=F32)
        carry = (jnp.full((1, tq), MASK_VALUE, F32), None)
        for kt in range(n_all):
            carry = half_step(kt, n_all, carry, qt=qt, q0=q0, masked=kt >= n_full)
        acc = jnp.dot(vt_ref[0, 0, n_all - 1], p_ref[(n_all - 1) % 2], preferred_element_type=F32)
        if n_all >= 2:
            acc = carry[1] * acc_ref[...] + acc
        o_ref[0, qi * tq:(qi + 1) * tq, :] = jnp.transpose(acc[:V_DIM] / acc[V_DIM:V_DIM + 1]).astype(BF16)


def _attention(qt, k3, vt, past, t_valid):
    b, _, nq, _, tq = qt.shape
    skv = k3.shape[1]
    nkt, tk = vt.shape[2], vt.shape[4]
    return pl.pallas_call(
        functools.partial(_attention_body, tq=tq, tk=tk, nkt=nkt, past=past, t_valid=t_valid),
        scratch_shapes=[pltpu.VMEM((2, tk, tq), F32), pltpu.VMEM((2, tk, tq), BF16), pltpu.VMEM((V_ROWS, tq), F32)],
        grid=(b, N_HEADS),
        in_specs=[pl.BlockSpec((1, 1, nq, HEAD_PAD, tq), lambda bi, h: (bi, h, 0, 0, 0)),
                  pl.BlockSpec((1, skv, HEAD_PAD), lambda bi, h: (bi, 0, h)),
                  pl.BlockSpec((1, 1, nkt, V_ROWS, tk), lambda bi, h: (bi, h, 0, 0, 0))],
        out_specs=pl.BlockSpec((1, nq * tq, V_DIM), lambda bi, h: (bi, 0, h)),
        out_shape=jax.ShapeDtypeStruct((b, nq * tq, N_HEADS * V_DIM), BF16),
        compiler_params=_params("parallel", "parallel"),
        name="attention",
    )(qt, k3, vt)


def _latent_attention_body(qn_ref, wq_ref, wuk_ref, cs_ref, klat_ref, wuv_ref, o_ref, q_ref, acc_ref,
                           *, ss, tk, n_tiles, past, t_valid, scale, rkv):
    qn = qn_ref[0]
    cs = cs_ref[...]
    lane = lax.broadcasted_iota(jnp.int32, (ss, LANES), 1)
    for h in range(N_HEADS):
        qh = jnp.dot(qn, wq_ref[:, h * HEAD_PAD:(h + 1) * HEAD_PAD], preferred_element_type=F32)
        q_abs = lax.dot_general(qh[:, :NOPE_DIM].astype(BF16), wuk_ref[:, h * NOPE_DIM:(h + 1) * NOPE_DIM],
                                (((1,), (1,)), ((), ())), preferred_element_type=F32)
        rope = jnp.where(lane < ROPE_DIM, _rope_fold(qh[:, NOPE_DIM:], cs), 0.0)
        q_ref[h * ss:(h + 1) * ss, 0:rkv] = (q_abs * scale).astype(BF16)
        q_ref[h * ss:(h + 1) * ss, rkv:rkv + LANES] = (rope * scale).astype(BF16)

    q = q_ref[...]
    cols = N_HEADS * ss
    q_chunk = (past + lax.broadcasted_iota(jnp.int32, (1, cols), 1) % ss) >> CHUNK_SHIFT
    acc_ref[...] = jnp.zeros((rkv, cols), F32)

    def step(kt, carry):
        m, l = carry
        tile = klat_ref[0, pl.ds(pl.multiple_of(kt * tk, tk), tk), :]
        s = lax.dot_general(tile, q, (((1,), (1,)), ((), ())), preferred_element_type=F32)
        k_pos = kt * tk + lax.broadcasted_iota(jnp.int32, (tk, 1), 0)
        k_chunk = jnp.where(k_pos < t_valid, k_pos >> CHUNK_SHIFT, jnp.iinfo(jnp.int32).max)
        s = jnp.where(k_chunk <= q_chunk, s, MASK_VALUE)
        m_new = jnp.maximum(m, _colmax(s))
        alpha = jnp.exp2(m - m_new)
        p = jnp.exp2(s - m_new)
        pv = lax.dot_general(tile[:, :rkv], p.astype(BF16), (((0,), (0,)), ((), ())), preferred_element_type=F32)
        acc_ref[...] = alpha * acc_ref[...] + pv
        return m_new, alpha * l + jnp.sum(p, axis=0, keepdims=True)

    init = (jnp.full((1, cols), MASK_VALUE, F32), jnp.zeros((1, cols), F32))
    _, l = lax.fori_loop(0, n_tiles, step, init)
    o_lat = jnp.transpose(acc_ref[...] / l).astype(BF16)
    for h in range(N_HEADS):
        o_ref[0, :, h * V_DIM:(h + 1) * V_DIM] = jnp.dot(
            o_lat[h * ss:(h + 1) * ss], wuv_ref[:, h * V_DIM:(h + 1) * V_DIM],
            preferred_element_type=F32).astype(BF16)


def _latent_attention(qn, wq, wuk, cs_tab, klat, wuv, past, t_valid, scale):
    b, ss, rq = qn.shape
    skv, feat = klat.shape[1], klat.shape[2]
    rkv = feat - LANES
    tk = min(TK_ATT, skv)
    see_any = min((((past + ss - 1) >> CHUNK_SHIFT) + 1) * CHUNK, t_valid)
    n_tiles = -(-see_any // tk)
    assert skv % tk == 0 and n_tiles * tk <= skv
    whole = lambda a: pl.BlockSpec(a.shape, lambda bi: (0,) * a.ndim)
    return pl.pallas_call(
        functools.partial(_latent_attention_body, ss=ss, tk=tk, n_tiles=n_tiles, past=past, t_valid=t_valid,
                          scale=scale, rkv=rkv),
        grid=(b,),
        in_specs=[pl.BlockSpec((1, ss, rq), lambda bi: (bi, 0, 0)), whole(wq), whole(wuk), whole(cs_tab),
                  pl.BlockSpec((1, skv, feat), lambda bi: (bi, 0, 0)), whole(wuv)],
        out_specs=pl.BlockSpec((1, ss, N_HEADS * V_DIM), lambda bi: (bi, 0, 0)),
        out_shape=jax.ShapeDtypeStruct((b, ss, N_HEADS * V_DIM), BF16),
        scratch_shapes=[pltpu.VMEM((N_HEADS * ss, feat), BF16), pltpu.VMEM((rkv, N_HEADS * ss), F32)],
        compiler_params=_params("parallel"),
        name="latent_attention",
    )(qn, wq, wuk, cs_tab, klat, wuv)


def _rope_table(past, s):
    half = ROPE_DIM // 2
    inv = ROPE_THETA ** (-jnp.arange(half, dtype=F32) / half)
    ang = (past + jnp.arange(s, dtype=jnp.int32)).astype(F32)[:, None] * inv[None, :]
    cos, sin = jnp.cos(ang), jnp.sin(ang)
    return jnp.concatenate([cos, cos, sin, sin], axis=-1)


def _rot_cols(w):
    half = ROPE_DIM // 2
    return jnp.concatenate([-w[..., half:], w[..., :half]], axis=-1)


def _pad_rows_front(x, rows):
    return jnp.pad(x, ((0, 0), (rows - x.shape[1], 0), (0, 0)))


def _even_layer(h, b, s, hist, wts, want_v):
    (g_mix, w_in, conv_w, conv_b, lag, lab, lvg, lvb, w_sp, b_sp_t, w_out) = wts
    z = _norm_matmul(h, g_mix, w_in, _tile(w_in.shape[1], TN_MATMUL))
    outs = _even_mid(z.reshape(b, s, -1), hist, conv_w, conv_b, lag, lab, lvg, lvb, w_sp, b_sp_t, want_v)
    cat, tail = outs[0], outs[1]
    h = _proj_residual([cat.reshape(b * s, -1)], [w_out], h)
    return h, tail[:, HALO_A - (CONV_A - 1):], (outs[2] if want_v else None)


def _odd_layer(h, b, s, hist, lat_prev, kr_prev, wts, cs_tab):
    (g_mix, w_in, conv_w, qg, wq, kvg, w_k, w_v, w_out_c, w_out_a) = wts
    dc = conv_w.shape[1]
    rq, rkv = qg.shape[1], kvg.shape[1]
    z = _norm_matmul(h, g_mix, w_in, w_in.shape[1] // 3)
    cout, qn, ckv, kr, tail = _odd_mid(z.reshape(b, s, -1), hist, conv_w, qg, kvg, cs_tab, dc, rq, rkv)
    scale = math.log2(math.e) / math.sqrt(NOPE_DIM + ROPE_DIM)
    if lat_prev is None:
        assert s % LANES == 0
        tk = min(TK_ATT, s)
        qt = _q_up(qn.reshape(b * s, rq), wq.T, cs_tab.T, b, s, min(TQ_ATT, s), scale)
        k, vt = _kv_up(ckv.reshape(b * s, rkv), kr.reshape(b * s, LANES), w_k, w_v.T, b, s, tk)
        attn = _attention(qt, k.reshape(b, s, -1), vt, 0, s)
    else:
        past = lat_prev.shape[1]
        t_valid = past + s
        tk = min(TK_ATT, t_valid)
        skv = -(-t_valid // tk) * tk
        kr_prev = jnp.pad(kr_prev, ((0, 0), (0, 0), (0, LANES - ROPE_DIM)))
        klat = jnp.concatenate([jnp.concatenate([lat_prev, ckv], axis=1),
                                jnp.concatenate([kr_prev, kr], axis=1)], axis=-1).astype(BF16)
        klat = jnp.pad(klat, ((0, 0), (0, skv - t_valid), (0, 0)))
        attn = _latent_attention(qn, wq, w_k, cs_tab, klat, w_v, past, t_valid, scale)
    h = _proj_residual([cout.reshape(b * s, dc), attn.reshape(b * s, -1)], [w_out_c, w_out_a], h)
    return h, tail[:, HALO_C - (CONV_C - 1):], ckv, kr[..., :ROPE_DIM]


def kernel(x_prompt, x_sample, state_conv_a, state_conv_c, cache_kv_latent, cache_k_rope, norm_mix, norm_ffn, norm_final, w_in_even, conv_a_w, conv_a_b, ln_a_g, ln_a_b, ln_v_g, ln_v_b, w_spatial, b_spatial, w_out_even, w_in_odd, conv_c_w, q_norm_g, w_uq, kv_norm_g, w_ukv, w_out_odd, w_ffn_up, w_ffn_down):
    bp, sp, d = x_prompt.shape
    bs, ss, _ = x_sample.shape
    depth = norm_mix.shape[0]
    da = conv_a_w.shape[-1]
    dc = conv_c_w.shape[-1]
    rq, rkv = q_norm_g.shape[-1], kv_norm_g.shape[-1]
    past = cache_kv_latent.shape[2]
    assert sp % GMLP_CHUNK == 0 and ss <= GMLP_CHUNK and ss >= HALO_A and past % GMLP_CHUNK == 0

    hp = x_prompt.reshape(bp * sp, d)
    hs = x_sample.reshape(bs * ss, d)
    cs_p = _rope_table(0, sp)
    cs_s = _rope_table(past, ss)
    wu_all, wd_all = _cast_bf16(w_ffn_up), _cast_bf16(w_ffn_down)
    w_in_even_bf, w_out_even_bf, w_out_odd_bf = _cast_bf16(w_in_even), _cast_bf16(w_out_even), _cast_bf16(w_out_odd)

    ca_p, ca_s, gv_s, cc_p, cc_s, lat_p, kr_p, lat_s, kr_s = [], [], [], [], [], [], [], [], []
    for l in range(depth):
        p = l // 2
        if l % 2 == 0:
            row = lambda a: a[p].reshape(1, -1)
            gl_p, gl_s = min(sp, GMLP_CHUNK), min(ss, GMLP_CHUNK)
            common = (norm_mix[l], w_in_even_bf[p],
                      jnp.pad(conv_a_w[p], ((0, HALO_A - CONV_A), (0, 0))), row(conv_a_b),
                      row(ln_a_g), row(ln_a_b), row(ln_v_g), row(ln_v_b))
            w_out = w_out_even_bf[p]
            wts_p = common + (w_spatial[p][:, :gl_p, :gl_p], b_spatial[p][:, :gl_p].T, w_out)
            wts_s = common + (w_spatial[p][:, :gl_s, :gl_s], b_spatial[p][:, :gl_s].T, w_out)
            hp, tail_p, _ = _even_layer(hp, bp, sp, jnp.zeros((bp, HALO_A, da), F32), wts_p, False)
            hs, tail_s, v_s = _even_layer(hs, bs, ss, _pad_rows_front(state_conv_a[p], HALO_A), wts_s, True)
            ca_p.append(tail_p)
            ca_s.append(tail_s)
            gv_s.append(v_s)
        else:
            w_in = w_in_odd[p]
            n_main = 3 * dc + rq + rkv
            w_in = jnp.concatenate([w_in, _rot_cols(w_in[:, n_main:])], axis=1).astype(BF16)
            wq = w_uq[p].reshape(rq, N_HEADS, NOPE_DIM + ROPE_DIM)
            wq = jnp.concatenate([wq, _rot_cols(wq[..., NOPE_DIM:])], axis=-1).reshape(rq, N_HEADS * HEAD_PAD)
            wq = wq.astype(BF16)
            wkv = w_ukv[p].reshape(rkv, N_HEADS, NOPE_DIM + V_DIM)
            wk = wkv[..., :NOPE_DIM].reshape(rkv, N_HEADS * NOPE_DIM).astype(BF16)
            wv = wkv[..., NOPE_DIM:].reshape(rkv, N_HEADS * V_DIM).astype(BF16)
            w_out = w_out_odd_bf[p]
            wts = (norm_mix[l], w_in, conv_c_w[p], q_norm_g[p].reshape(1, rq), wq,
                   kv_norm_g[p].reshape(1, rkv), wk, wv, w_out[:dc], w_out[dc:])
            hp, tail_p, c_p, r_p = _odd_layer(hp, bp, sp, jnp.zeros((bp, HALO_C, dc), F32), None, None,
                                              wts, cs_p)
            hs, tail_s, c_s, r_s = _odd_layer(hs, bs, ss, _pad_rows_front(state_conv_c[p], HALO_C),
                                              cache_kv_latent[p], cache_k_rope[p], wts, cs_s)
            cc_p.append(tail_p)
            cc_s.append(tail_s)
            lat_p.append(c_p)
            kr_p.append(r_p)
            lat_s.append(c_s)
            kr_s.append(r_s)
        last = l == depth - 1
        hp = _ffn(hp, norm_ffn[l], wu_all, wd_all, l, norm_final, last)
        hs = _ffn(hs, norm_ffn[l], wu_all, wd_all, l, norm_final, last)

    return (hp.reshape(bp, sp, d), hs.reshape(bs, ss, d),
            jnp.stack(ca_p), jnp.stack(ca_s), jnp.stack(gv_s), jnp.stack(cc_p), jnp.stack(cc_s),
            jnp.stack(lat_p), jnp.stack(kr_p), jnp.stack(lat_s), jnp.stack(kr_s))
```

```python
import functools
import math

import jax
import jax.numpy as jnp
from jax import lax
from jax.experimental import pallas as pl
from jax.experimental.pallas import tpu as pltpu

F32 = jnp.float32
BF16 = jnp.bfloat16

CHUNK = 64
CHUNK_SHIFT = 6
CONV_A = 31
CONV_C = 3
G_B = 8
GMLP_CHUNK = 128
N_HEADS = 16
NOPE_DIM = 128
ROPE_DIM = 64
V_DIM = 128
V_ROWS = V_DIM + 16
ROPE_THETA = 10000.0
EPS = 1e-6

LANES = 128
SUBLANES = 8
HEAD_PAD = 2 * LANES
HALO_A = 32
HALO_C = 8
MASK_VALUE = -1e30
VMEM_LIMIT = 60 * 1024 * 1024

TM_MATMUL = 1024
TN_MATMUL = 1024
TF_FFN = 1024
TS_EVEN = 256
TS_ODD = 512
TQ_ATT = 512
TK_ATT = 512
CAST_BLOCK = (1024, 2048)
ATT_GROUP = 4
KV_ROWS_STEP = 1024


def _params(*sem):
    return pltpu.CompilerParams(dimension_semantics=sem, vmem_limit_bytes=VMEM_LIMIT)


def _rms(x, g):
    return x * lax.rsqrt(jnp.mean(x * x, axis=-1, keepdims=True) + EPS) * g


def _layer_norm(x, g, b):
    xc = x - jnp.mean(x, axis=-1, keepdims=True)
    return xc * lax.rsqrt(jnp.mean(xc * xc, axis=-1, keepdims=True) + EPS) * g + b


def _sigmoid(x):
    return 0.5 * jnp.tanh(0.5 * x) + 0.5


def _gelu(x):
    return 0.5 * x * (1.0 + lax.erf(x * math.sqrt(0.5)))


def _tile(n, t):
    t = min(n, t)
    assert n % t == 0, (n, t)
    return t


def _norm_matmul_body(x_ref, g_ref, w_ref, o_ref, xn_ref):
    @pl.when(pl.program_id(1) == 0)
    def _():
        xn_ref[...] = _rms(x_ref[...], g_ref[...]).astype(BF16)

    o_ref[...] = jnp.dot(xn_ref[...], w_ref[...], preferred_element_type=F32)


def _norm_matmul(x, g, w, tn):
    t, d = x.shape
    n = w.shape[1]
    tm = _tile(t, TM_MATMUL)
    return pl.pallas_call(
        _norm_matmul_body,
        grid=(t // tm, n // tn),
        in_specs=[pl.BlockSpec((tm, d), lambda i, j: (i, 0)),
                  pl.BlockSpec((1, d), lambda i, j: (0, 0)),
                  pl.BlockSpec((d, tn), lambda i, j: (0, j))],
        out_specs=pl.BlockSpec((tm, tn), lambda i, j: (i, j)),
        out_shape=jax.ShapeDtypeStruct((t, n), F32),
        scratch_shapes=[pltpu.VMEM((tm, d), BF16)],
        compiler_params=_params("parallel", "arbitrary"),
        name="norm_matmul",
    )(x, g.reshape(1, d), w)


def _even_mid_body(*refs, ts, da, gl, want_v):
    (zc_ref, zh_ref, hist_ref, cw_ref, cb_ref, lag_ref, lab_ref, lvg_ref, lvb_ref,
     wsp_ref, bsp_ref, cat_ref, tail_ref) = refs[:13]
    v_ref = refs[13] if want_v else None
    buf_ref, conv_ref = refs[-2:]
    i = pl.program_id(1)

    buf_ref[HALO_A:HALO_A + ts, :] = zc_ref[0, :, 0:da] * _sigmoid(zc_ref[0, :, da:2 * da])

    @pl.when(i == 0)
    def _():
        buf_ref[0:HALO_A, :] = hist_ref[0]

    @pl.when(i > 0)
    def _():
        buf_ref[0:HALO_A, :] = zh_ref[0, :, 0:da] * _sigmoid(zh_ref[0, :, da:2 * da])

    tail_ref[0] = buf_ref[ts:ts + HALO_A, :]

    rc = min(ts, 128)
    base = HALO_A - (CONV_A - 1)
    win = rc + HALO_A
    for c in range(da // LANES):
        cs = slice(c * LANES, (c + 1) * LANES)
        for r0 in range(0, ts, rc):
            window = buf_ref[r0:r0 + win, cs]
            acc = jnp.broadcast_to(cb_ref[:, cs], (rc, LANES))
            for sh in range(SUBLANES):
                rolled = window if sh == 0 else pltpu.roll(window, win - sh, axis=0)
                for off in range(sh, HALO_A + 1, SUBLANES):
                    j = off - base
                    if 0 <= j < CONV_A:
                        acc = acc + cw_ref[j:j + 1, cs] * rolled[off - sh:off - sh + rc]
            conv_ref[r0:r0 + rc, cs] = acc

    y = _layer_norm(conv_ref[...], lag_ref[...], lab_ref[...])
    cat_ref[0, :, 0:da] = (y * _sigmoid(y)).astype(BF16)

    u = _gelu(zc_ref[0, :, 2 * da:3 * da])
    v = _layer_norm(_gelu(zc_ref[0, :, 3 * da:4 * da]), lvg_ref[...], lvb_ref[...])
    if want_v:
        v_ref[0] = v
    vb = v.astype(BF16)
    dh = da // G_B
    tril = lax.broadcasted_iota(jnp.int32, (gl, gl), 0) >= lax.broadcasted_iota(jnp.int32, (gl, gl), 1)
    for g in range(G_B):
        wg = jnp.where(tril, wsp_ref[g], 0.0).astype(BF16)
        gs = slice(g * dh, (g + 1) * dh)
        for k in range(ts // gl):
            rs = slice(k * gl, (k + 1) * gl)
            mixed = jnp.dot(wg, vb[rs, gs], preferred_element_type=F32) + bsp_ref[:, g:g + 1]
            cat_ref[0, rs, da + g * dh:da + (g + 1) * dh] = (u[rs, gs] * mixed).astype(BF16)


def _even_mid(z3, hist, conv_w, conv_b, lag, lab, lvg, lvb, w_sp, b_sp_t, want_v):
    b, s, n4 = z3.shape
    da = n4 // 4
    gl = w_sp.shape[-1]
    ts = _tile(s, TS_EVEN)
    assert ts % gl == 0 and ts % HALO_A == 0
    hb = ts // HALO_A
    row = lambda bi, i: (0, 0)
    out_shape = [jax.ShapeDtypeStruct((b, s, 2 * da), BF16), jax.ShapeDtypeStruct((b, HALO_A, da), F32)]
    out_specs = [pl.BlockSpec((1, ts, 2 * da), lambda bi, i: (bi, i, 0)),
                 pl.BlockSpec((1, HALO_A, da), lambda bi, i: (bi, 0, 0))]
    if want_v:
        out_shape.append(jax.ShapeDtypeStruct((b, s, da), F32))
        out_specs.append(pl.BlockSpec((1, ts, da), lambda bi, i: (bi, i, 0)))
    return pl.pallas_call(
        functools.partial(_even_mid_body, ts=ts, da=da, gl=gl, want_v=want_v),
        grid=(b, s // ts),
        in_specs=[pl.BlockSpec((1, ts, n4), lambda bi, i: (bi, i, 0)),
                  pl.BlockSpec((1, HALO_A, 2 * da), lambda bi, i: (bi, jnp.maximum(i * hb - 1, 0), 0)),
                  pl.BlockSpec((1, HALO_A, da), lambda bi, i: (bi, 0, 0)),
                  pl.BlockSpec((HALO_A, da), row),
                  pl.BlockSpec((1, da), row), pl.BlockSpec((1, da), row), pl.BlockSpec((1, da), row),
                  pl.BlockSpec((1, da), row), pl.BlockSpec((1, da), row),
                  pl.BlockSpec((G_B, gl, gl), lambda bi, i: (0, 0, 0)),
                  pl.BlockSpec((gl, G_B), row)],
        out_specs=out_specs,
        out_shape=out_shape,
        scratch_shapes=[pltpu.VMEM((HALO_A + ts, da), F32), pltpu.VMEM((ts, da), F32)],
        compiler_params=_params("parallel", "arbitrary"),
        name="even_mid",
    )(z3, z3, hist, conv_w, conv_b, lag, lab, lvg, lvb, w_sp, b_sp_t)


def _proj_residual_body(*refs, n_in):
    x_refs, w_refs = refs[:n_in], refs[n_in:2 * n_in]
    r_ref, o_ref = refs[2 * n_in], refs[2 * n_in + 1]
    acc = r_ref[...]
    for x_ref, w_ref in zip(x_refs, w_refs):
        acc = acc + jnp.dot(x_ref[...], w_ref[...], preferred_element_type=F32)
    o_ref[...] = acc


def _proj_residual(xs, ws, res):
    t, n = res.shape
    tm = _tile(t, TM_MATMUL)
    tn = _tile(n, TN_MATMUL)
    in_specs = [pl.BlockSpec((tm, x.shape[1]), lambda i, j: (i, 0)) for x in xs]
    in_specs += [pl.BlockSpec((w.shape[0], tn), lambda i, j: (0, j)) for w in ws]
    in_specs.append(pl.BlockSpec((tm, tn), lambda i, j: (i, j)))
    return pl.pallas_call(
        functools.partial(_proj_residual_body, n_in=len(xs)),
        grid=(t // tm, n // tn),
        in_specs=in_specs,
        out_specs=pl.BlockSpec((tm, tn), lambda i, j: (i, j)),
        out_shape=jax.ShapeDtypeStruct((t, n), F32),
        compiler_params=_params("parallel", "arbitrary"),
        name="proj_residual",
    )(*xs, *ws, res)


def _ffn_body(x_ref, g_ref, wu_ref, wd_ref, gf_ref, o_ref, xn_ref, *, final_norm):
    j = pl.program_id(1)

    @pl.when(j == 0)
    def _():
        x = x_ref[...]
        xn_ref[...] = _rms(x, g_ref[...]).astype(BF16)
        o_ref[...] = x

    a = jnp.maximum(jnp.dot(xn_ref[...], wu_ref[...], preferred_element_type=F32), 0.0)
    o_ref[...] += jnp.dot((a * a).astype(BF16), wd_ref[...], preferred_element_type=F32)

    if final_norm:
        @pl.when(j == pl.num_programs(1) - 1)
        def _():
            o_ref[...] = _rms(o_ref[...], gf_ref[...])


def _cast_bf16_body(w_ref, o_ref):
    o_ref[...] = w_ref[...].astype(BF16)


def _cast_bf16(w):
    nl, r, c = w.shape
    tr, tc = _tile(r, CAST_BLOCK[0]), _tile(c, CAST_BLOCK[1])
    spec = pl.BlockSpec((1, tr, tc), lambda l, i, j: (l, i, j))
    return pl.pallas_call(
        _cast_bf16_body,
        grid=(nl, r // tr, c // tc),
        in_specs=[spec],
        out_specs=spec,
        out_shape=jax.ShapeDtypeStruct(w.shape, BF16),
        compiler_params=_params("parallel", "parallel", "parallel"),
        name="cast_bf16",
    )(w)


def _ffn(x, g, wu, wd, layer, gf, final_norm):
    t, d = x.shape
    f = wu.shape[2]
    tm = _tile(t, TM_MATMUL)
    tf = _tile(f, TF_FFN)
    return pl.pallas_call(
        functools.partial(_ffn_body, final_norm=final_norm),
        grid=(t // tm, f // tf),
        in_specs=[pl.BlockSpec((tm, d), lambda i, j: (i, 0)),
                  pl.BlockSpec((1, d), lambda i, j: (0, 0)),
                  pl.BlockSpec((None, d, tf), lambda i, j: (layer, 0, j)),
                  pl.BlockSpec((None, tf, d), lambda i, j: (layer, j, 0)),
                  pl.BlockSpec((1, d), lambda i, j: (0, 0))],
        out_specs=pl.BlockSpec((tm, d), lambda i, j: (i, 0)),
        out_shape=jax.ShapeDtypeStruct((t, d), F32),
        scratch_shapes=[pltpu.VMEM((tm, d), BF16)],
        compiler_params=_params("parallel", "arbitrary"),
        name="ffn",
    )(x, g.reshape(1, d), wu, wd, gf.reshape(1, d))


def _rope_fold(y, cs):
    t = y * cs
    return t + pltpu.roll(t, ROPE_DIM, axis=1)


def _q_heads_store(wq_ref, qn, cs_t, qt_ref, scale):
    hpg = 4
    for g in range(N_HEADS // hpg):
        acc = lax.dot_general(wq_ref[g * hpg * HEAD_PAD:(g + 1) * hpg * HEAD_PAD, :], qn,
                              (((1,), (1,)), ((), ())), preferred_element_type=F32)
        for hh in range(hpg):
            h, lo = g * hpg + hh, hh * HEAD_PAD
            qt_ref[0, h, 0, 0:NOPE_DIM, :] = (acc[lo:lo + NOPE_DIM] * scale).astype(BF16)
            t = acc[lo + NOPE_DIM:lo + HEAD_PAD] * cs_t
            qt_ref[0, h, 0, NOPE_DIM:NOPE_DIM + ROPE_DIM, :] = ((t[:ROPE_DIM] + t[ROPE_DIM:]) * scale).astype(BF16)
            qt_ref[0, h, 0, NOPE_DIM + ROPE_DIM:HEAD_PAD, :] = jnp.zeros((ROPE_DIM, acc.shape[1]), BF16)


def _odd_mid_body(*refs, ts, dc, rq, rkv, q_scale):
    zc_ref, hc_ref, hx_ref, hist_ref, cw_ref, qg_ref, kvg_ref, cs_ref = refs[:8]
    if q_scale is not None:
        wq_ref, cst_ref = refs[8:10]
    cout_ref, q_out_ref, ckv_ref, kr_ref, tail_ref, buf_ref = refs[-6:]
    i = pl.program_id(1)
    gx = zc_ref[0, :, dc:2 * dc] * zc_ref[0, :, 2 * dc:3 * dc]
    buf_ref[HALO_C:HALO_C + ts, :] = gx

    @pl.when(i == 0)
    def _():
        buf_ref[0:HALO_C, :] = hist_ref[0]

    @pl.when(i > 0)
    def _():
        buf_ref[0:HALO_C, :] = hc_ref[0] * hx_ref[0]

    tail_ref[0] = buf_ref[ts:ts + HALO_C, :]
    conv = cw_ref[CONV_C - 1:CONV_C, :] * gx
    for j in range(CONV_C - 1):
        off = HALO_C - (CONV_C - 1) + j
        conv = conv + cw_ref[j:j + 1, :] * buf_ref[off:off + ts, :]
    cout_ref[0] = (zc_ref[0, :, 0:dc] * conv).astype(BF16)

    o = 3 * dc
    qn = _rms(zc_ref[0, :, o:o + rq], qg_ref[...]).astype(BF16)
    if q_scale is None:
        q_out_ref[0] = qn
    else:
        _q_heads_store(wq_ref, qn, cst_ref[...], q_out_ref, q_scale)
    ckv_ref[0] = _rms(zc_ref[0, :, o + rq:o + rq + rkv], kvg_ref[...])
    kr = _rope_fold(zc_ref[0, :, o + rq + rkv:o + rq + rkv + LANES], cs_ref[...])
    lane = lax.broadcasted_iota(jnp.int32, kr.shape, 1)
    kr_ref[0] = jnp.where(lane < ROPE_DIM, kr, 0.0)


def _odd_mid(z3, hist, conv_w, qg, kvg, cs_tab, dc, rq, rkv, wq_t=None, q_scale=None):
    b, s, n = z3.shape
    ts = _tile(s, TS_ODD)
    assert ts % HALO_C == 0 and n == 3 * dc + rq + rkv + LANES and dc % 1024 == 0
    hb = ts // HALO_C
    halo = lambda col: pl.BlockSpec((1, HALO_C, dc), lambda bi, i: (bi, jnp.maximum(i * hb - 1, 0), col))
    row = lambda bi, i: (0, 0)
    tok = lambda w: pl.BlockSpec((1, ts, w), lambda bi, i: (bi, i, 0))
    args = [z3, z3, z3, hist, conv_w, qg, kvg, cs_tab]
    in_specs = [tok(n), halo(1), halo(2),
                pl.BlockSpec((1, HALO_C, dc), lambda bi, i: (bi, 0, 0)),
                pl.BlockSpec((CONV_C, dc), row),
                pl.BlockSpec((1, rq), row), pl.BlockSpec((1, rkv), row),
                pl.BlockSpec((ts, LANES), lambda bi, i: (i, 0))]
    if wq_t is None:
        q_spec, q_shape = tok(rq), jax.ShapeDtypeStruct((b, s, rq), BF16)
    else:
        args += [wq_t, cs_tab.T]
        in_specs += [pl.BlockSpec(wq_t.shape, row), pl.BlockSpec((LANES, ts), lambda bi, i: (0, i))]
        q_spec = pl.BlockSpec((1, N_HEADS, 1, HEAD_PAD, ts), lambda bi, i: (bi, 0, i, 0, 0))
        q_shape = jax.ShapeDtypeStruct((b, N_HEADS, s // ts, HEAD_PAD, ts), BF16)
    return pl.pallas_call(
        functools.partial(_odd_mid_body, ts=ts, dc=dc, rq=rq, rkv=rkv, q_scale=q_scale),
        grid=(b, s // ts),
        in_specs=in_specs,
        out_specs=[tok(dc), q_spec, tok(rkv), tok(LANES),
                   pl.BlockSpec((1, HALO_C, dc), lambda bi, i: (bi, 0, 0))],
        out_shape=[jax.ShapeDtypeStruct((b, s, dc), BF16),
                   q_shape,
                   jax.ShapeDtypeStruct((b, s, rkv), F32),
                   jax.ShapeDtypeStruct((b, s, LANES), F32),
                   jax.ShapeDtypeStruct((b, HALO_C, dc), F32)],
        scratch_shapes=[pltpu.VMEM((HALO_C + ts, dc), F32)],
        compiler_params=_params("parallel", "arbitrary"),
        name="odd_mid",
    )(*args)


def _kv_up_body(c_ref, kr_ref, wk_ref, wvt_ref, k_ref, vt_ref):
    c = c_ref[...].astype(BF16)
    k = jnp.dot(c, wk_ref[...], preferred_element_type=F32)
    vt = lax.dot_general(wvt_ref[...], c, (((1,), (1,)), ((), ())), preferred_element_type=F32)
    kr = kr_ref[...].astype(BF16)
    tk = vt_ref.shape[-1]
    for h in range(k.shape[1] // NOPE_DIM):
        k_ref[:, h * HEAD_PAD:h * HEAD_PAD + NOPE_DIM] = k[:, h * NOPE_DIM:(h + 1) * NOPE_DIM].astype(BF16)
        k_ref[:, h * HEAD_PAD + NOPE_DIM:(h + 1) * HEAD_PAD] = kr
        for t in range(vt_ref.shape[2]):
            vt_ref[0, h, t, 0:V_DIM, :] = vt[h * V_DIM:(h + 1) * V_DIM, t * tk:(t + 1) * tk].astype(BF16)
            vt_ref[0, h, t, V_DIM:V_ROWS, :] = jnp.ones((V_ROWS - V_DIM, tk), BF16)


def _kv_up(c, kr, wk, wv_t, b, skv, tk):
    t, r = c.shape
    nkt = skv // tk
    tm = _tile(skv, KV_ROWS_STEP)
    kts = tm // tk
    per = skv // tm
    hpt = 4
    return pl.pallas_call(
        _kv_up_body,
        grid=(t // tm, N_HEADS // hpt),
        in_specs=[pl.BlockSpec((tm, r), lambda i, j: (i, 0)),
                  pl.BlockSpec((tm, LANES), lambda i, j: (i, 0)),
                  pl.BlockSpec((r, hpt * NOPE_DIM), lambda i, j: (0, j)),
                  pl.BlockSpec((hpt * V_DIM, r), lambda i, j: (j, 0))],
        out_specs=[pl.BlockSpec((tm, hpt * HEAD_PAD), lambda i, j: (i, j)),
                   pl.BlockSpec((1, hpt, kts, V_ROWS, tk), lambda i, j: (i // per, j, i % per, 0, 0))],
        out_shape=[jax.ShapeDtypeStruct((t, N_HEADS * HEAD_PAD), BF16),
                   jax.ShapeDtypeStruct((b, N_HEADS, nkt, V_ROWS, tk), BF16)],
        compiler_params=_params("parallel", "arbitrary"),
        name="kv_up",
    )(c, kr, wk, wv_t)


def _colmax(s):
    while s.shape[0] > 8 and s.shape[0] % 16 == 0:
        half = s.shape[0] // 2
        s = jnp.maximum(s[:half], s[half:])
    return jnp.max(s, axis=0, keepdims=True)


def _attention_body(qt_ref, k_ref, vt_ref, o_ref, s_ref, p_ref, acc_ref, *, tq, tk, nkt, past, t_valid):
    def k_tile(kt):
        return k_ref[0, kt * tk:(kt + 1) * tk, :]

    def half_step(kt, n_steps, carry, qt, q0, masked):
        m, alpha = carry
        slot, other = kt % 2, 1 - kt % 2
        if kt >= 1:
            pv = jnp.dot(vt_ref[0, 0, kt - 1], p_ref[other], preferred_element_type=F32)
        if kt + 1 < n_steps:
            s_next = jnp.dot(k_tile(kt + 1), qt, preferred_element_type=F32)
        s = s_ref[slot]
        if masked:
            k_pos = kt * tk + lax.broadcasted_iota(jnp.int32, (tk, 1), 0)
            q_pos = q0 + lax.broadcasted_iota(jnp.int32, (1, tq), 1)
            k_chunk = jnp.where(k_pos < t_valid, k_pos >> CHUNK_SHIFT, jnp.iinfo(jnp.int32).max)
            s = jnp.where(k_chunk <= q_pos >> CHUNK_SHIFT, s, MASK_VALUE)
        m_new = jnp.maximum(m, _colmax(s))
        p_ref[slot] = jnp.exp2(s - m_new).astype(BF16)
        if kt == 1:
            acc_ref[...] = pv
        elif kt > 1:
            acc_ref[...] = alpha * acc_ref[...] + pv
        if kt + 1 < n_steps:
            s_ref[other] = s_next
        return m_new, jnp.exp2(m - m_new)

    for qi in range(qt_ref.shape[2]):
        qt = qt_ref[0, 0, qi]
        q0 = past + qi * tq
        n_full = min(((q0 >> CHUNK_SHIFT) + 1) * CHUNK, t_valid) // tk
        n_all = -(-min((((q0 + tq - 1) >> CHUNK_SHIFT) + 1) * CHUNK, t_valid) // tk)
        s_ref[0] = jnp.dot(k_tile(0), qt, preferred_element_type=F32)
        carry = (jnp.full((1, tq), MASK_VALUE, F32), None)
        for kt in range(n_all):
            carry = half_step(kt, n_all, carry, qt=qt, q0=q0, masked=kt >= n_full)
        acc = jnp.dot(vt_ref[0, 0, n_all - 1], p_ref[(n_all - 1) % 2], preferred_element_type=F32)
        if n_all >= 2:
            acc = carry[1] * acc_ref[...] + acc
        o_ref[0, qi * tq:(qi + 1) * tq, :] = jnp.transpose(acc[:V_DIM] / acc[V_DIM:V_DIM + 1]).astype(BF16)


def _attention(qt, k3, vt, past, t_valid):
    b, _, nq, _, tq = qt.shape
    skv = k3.shape[1]
    nkt, tk = vt.shape[2], vt.shape[4]
    return pl.pallas_call(
        functools.partial(_attention_body, tq=tq, tk=tk, nkt=nkt, past=past, t_valid=t_valid),
        scratch_shapes=[pltpu.VMEM((2, tk, tq), F32), pltpu.VMEM((2, tk, tq), BF16), pltpu.VMEM((V_ROWS, tq), F32)],
        grid=(b, N_HEADS),
        in_specs=[pl.BlockSpec((1, 1, nq, HEAD_PAD, tq), lambda bi, h: (bi, h, 0, 0, 0)),
                  pl.BlockSpec((1, skv, HEAD_PAD), lambda bi, h: (bi, 0, h)),
                  pl.BlockSpec((1, 1, nkt, V_ROWS, tk), lambda bi, h: (bi, h, 0, 0, 0))],
        out_specs=pl.BlockSpec((1, nq * tq, V_DIM), lambda bi, h: (bi, 0, h)),
        out_shape=jax.ShapeDtypeStruct((b, nq * tq, N_HEADS * V_DIM), BF16),
        compiler_params=_params("parallel", "parallel"),
        name="attention",
    )(qt, k3, vt)


def _latent_attention_body(qn_ref, wq_ref, wuk_ref, cs_ref, klat_ref, wuv_ref, o_ref, q_ref, acc_ref,
                           *, ss, tk, n_tiles, past, t_valid, scale, rkv):
    qn = qn_ref[0]
    cs = cs_ref[...]
    lane = lax.broadcasted_iota(jnp.int32, (ss, LANES), 1)
    for h in range(N_HEADS):
        qh = jnp.dot(qn, wq_ref[:, h * HEAD_PAD:(h + 1) * HEAD_PAD], preferred_element_type=F32)
        q_abs = lax.dot_general(qh[:, :NOPE_DIM].astype(BF16), wuk_ref[:, h * NOPE_DIM:(h + 1) * NOPE_DIM],
                                (((1,), (1,)), ((), ())), preferred_element_type=F32)
        rope = jnp.where(lane < ROPE_DIM, _rope_fold(qh[:, NOPE_DIM:], cs), 0.0)
        q_ref[h * ss:(h + 1) * ss, 0:rkv] = (q_abs * scale).astype(BF16)
        q_ref[h * ss:(h + 1) * ss, rkv:rkv + LANES] = (rope * scale).astype(BF16)

    q = q_ref[...]
    cols = N_HEADS * ss
    q_chunk = (past + lax.broadcasted_iota(jnp.int32, (1, cols), 1) % ss) >> CHUNK_SHIFT
    acc_ref[...] = jnp.zeros((rkv, cols), F32)

    def step(kt, carry):
        m, l = carry
        tile = klat_ref[0, pl.ds(pl.multiple_of(kt * tk, tk), tk), :]
        s = lax.dot_general(tile, q, (((1,), (1,)), ((), ())), preferred_element_type=F32)
        k_pos = kt * tk + lax.broadcasted_iota(jnp.int32, (tk, 1), 0)
        k_chunk = jnp.where(k_pos < t_valid, k_pos >> CHUNK_SHIFT, jnp.iinfo(jnp.int32).max)
        s = jnp.where(k_chunk <= q_chunk, s, MASK_VALUE)
        m_new = jnp.maximum(m, _colmax(s))
        alpha = jnp.exp2(m - m_new)
        p = jnp.exp2(s - m_new)
        pv = lax.dot_general(tile[:, :rkv], p.astype(BF16), (((0,), (0,)), ((), ())), preferred_element_type=F32)
        acc_ref[...] = alpha * acc_ref[...] + pv
        return m_new, alpha * l + jnp.sum(p, axis=0, keepdims=True)

    init = (jnp.full((1, cols), MASK_VALUE, F32), jnp.zeros((1, cols), F32))
    _, l = lax.fori_loop(0, n_tiles, step, init)
    o_lat = jnp.transpose(acc_ref[...] / l).astype(BF16)
    for h in range(N_HEADS):
        o_ref[0, :, h * V_DIM:(h + 1) * V_DIM] = jnp.dot(
            o_lat[h * ss:(h + 1) * ss], wuv_ref[:, h * V_DIM:(h + 1) * V_DIM],
            preferred_element_type=F32).astype(BF16)


def _latent_attention(qn, wq, wuk, cs_tab, klat, wuv, past, t_valid, scale):
    b, ss, rq = qn.shape
    skv, feat = klat.shape[1], klat.shape[2]
    rkv = feat - LANES
    tk = min(TK_ATT, skv)
    see_any = min((((past + ss - 1) >> CHUNK_SHIFT) + 1) * CHUNK, t_valid)
    n_tiles = -(-see_any // tk)
    assert skv % tk == 0 and n_tiles * tk <= skv
    whole = lambda a: pl.BlockSpec(a.shape, lambda bi: (0,) * a.ndim)
    return pl.pallas_call(
        functools.partial(_latent_attention_body, ss=ss, tk=tk, n_tiles=n_tiles, past=past, t_valid=t_valid,
                          scale=scale, rkv=rkv),
        grid=(b,),
        in_specs=[pl.BlockSpec((1, ss, rq), lambda bi: (bi, 0, 0)), whole(wq), whole(wuk), whole(cs_tab),
                  pl.BlockSpec((1, skv, feat), lambda bi: (bi, 0, 0)), whole(wuv)],
        out_specs=pl.BlockSpec((1, ss, N_HEADS * V_DIM), lambda bi: (bi, 0, 0)),
        out_shape=jax.ShapeDtypeStruct((b, ss, N_HEADS * V_DIM), BF16),
        scratch_shapes=[pltpu.VMEM((N_HEADS * ss, feat), BF16), pltpu.VMEM((rkv, N_HEADS * ss), F32)],
        compiler_params=_params("parallel"),
        name="latent_attention",
    )(qn, wq, wuk, cs_tab, klat, wuv)


def _rope_table(past, s):
    half = ROPE_DIM // 2
    inv = ROPE_THETA ** (-jnp.arange(half, dtype=F32) / half)
    ang = (past + jnp.arange(s, dtype=jnp.int32)).astype(F32)[:, None] * inv[None, :]
    cos, sin = jnp.cos(ang), jnp.sin(ang)
    return jnp.concatenate([cos, cos, sin, sin], axis=-1)


def _rot_cols(w):
    half = ROPE_DIM // 2
    return jnp.concatenate([-w[..., half:], w[..., :half]], axis=-1)


def _pad_rows_front(x, rows):
    return jnp.pad(x, ((0, 0), (rows - x.shape[1], 0), (0, 0)))


def _even_layer(h, b, s, hist, wts, want_v):
    (g_mix, w_in, conv_w, conv_b, lag, lab, lvg, lvb, w_sp, b_sp_t, w_out) = wts
    z = _norm_matmul(h, g_mix, w_in, _tile(w_in.shape[1], TN_MATMUL))
    outs = _even_mid(z.reshape(b, s, -1), hist, conv_w, conv_b, lag, lab, lvg, lvb, w_sp, b_sp_t, want_v)
    cat, tail = outs[0], outs[1]
    h = _proj_residual([cat.reshape(b * s, -1)], [w_out], h)
    return h, tail[:, HALO_A - (CONV_A - 1):], (outs[2] if want_v else None)


def _odd_layer(h, b, s, hist, lat_prev, kr_prev, wts, cs_tab):
    (g_mix, w_in, conv_w, qg, wq, kvg, w_k, w_v, w_out_c, w_out_a) = wts
    dc = conv_w.shape[1]
    rq, rkv = qg.shape[1], kvg.shape[1]
    z = _norm_matmul(h, g_mix, w_in, w_in.shape[1] // 3)
    scale = math.log2(math.e) / math.sqrt(NOPE_DIM + ROPE_DIM)
    if lat_prev is None:
        assert s % LANES == 0 and TS_ODD == TQ_ATT
        cout, qt, ckv, kr, tail = _odd_mid(z.reshape(b, s, -1), hist, conv_w, qg, kvg, cs_tab, dc, rq, rkv,
                                           wq.T, scale)
        tk = min(TK_ATT, s)
        k, vt = _kv_up(ckv.reshape(b * s, rkv), kr.reshape(b * s, LANES), w_k, w_v.T, b, s, tk)
        attn = _attention(qt, k.reshape(b, s, -1), vt, 0, s)
    else:
        cout, qn, ckv, kr, tail = _odd_mid(z.reshape(b, s, -1), hist, conv_w, qg, kvg, cs_tab, dc, rq, rkv)
        past = lat_prev.shape[1]
        t_valid = past + s
        tk = min(TK_ATT, t_valid)
        skv = -(-t_valid // tk) * tk
        kr_prev = jnp.pad(kr_prev, ((0, 0), (0, 0), (0, LANES - ROPE_DIM)))
        klat = jnp.concatenate([jnp.concatenate([lat_prev, ckv], axis=1),
                                jnp.concatenate([kr_prev, kr], axis=1)], axis=-1).astype(BF16)
        klat = jnp.pad(klat, ((0, 0), (0, skv - t_valid), (0, 0)))
        attn = _latent_attention(qn, wq, w_k, cs_tab, klat, w_v, past, t_valid, scale)
    h = _proj_residual([cout.reshape(b * s, dc), attn.reshape(b * s, -1)], [w_out_c, w_out_a], h)
    return h, tail[:, HALO_C - (CONV_C - 1):], ckv, kr[..., :ROPE_DIM]


def kernel(x_prompt, x_sample, state_conv_a, state_conv_c, cache_kv_latent, cache_k_rope, norm_mix, norm_ffn, norm_final, w_in_even, conv_a_w, conv_a_b, ln_a_g, ln_a_b, ln_v_g, ln_v_b, w_spatial, b_spatial, w_out_even, w_in_odd, conv_c_w, q_norm_g, w_uq, kv_norm_g, w_ukv, w_out_odd, w_ffn_up, w_ffn_down):
    bp, sp, d = x_prompt.shape
    bs, ss, _ = x_sample.shape
    depth = norm_mix.shape[0]
    da = conv_a_w.shape[-1]
    dc = conv_c_w.shape[-1]
    rq, rkv = q_norm_g.shape[-1], kv_norm_g.shape[-1]
    past = cache_kv_latent.shape[2]
    assert sp % GMLP_CHUNK == 0 and ss <= GMLP_CHUNK and ss >= HALO_A and past % GMLP_CHUNK == 0

    hp = x_prompt.reshape(bp * sp, d)
    hs = x_sample.reshape(bs * ss, d)
    cs_p = _rope_table(0, sp)
    cs_s = _rope_table(past, ss)
    wu_all, wd_all = _cast_bf16(w_ffn_up), _cast_bf16(w_ffn_down)
    w_in_even_bf, w_out_even_bf, w_out_odd_bf = _cast_bf16(w_in_even), _cast_bf16(w_out_even), _cast_bf16(w_out_odd)

    ca_p, ca_s, gv_s, cc_p, cc_s, lat_p, kr_p, lat_s, kr_s = [], [], [], [], [], [], [], [], []
    for l in range(depth):
        p = l // 2
        if l % 2 == 0:
            row = lambda a: a[p].reshape(1, -1)
            gl_p, gl_s = min(sp, GMLP_CHUNK), min(ss, GMLP_CHUNK)
            common = (norm_mix[l], w_in_even_bf[p],
                      jnp.pad(conv_a_w[p], ((0, HALO_A - CONV_A), (0, 0))), row(conv_a_b),
                      row(ln_a_g), row(ln_a_b), row(ln_v_g), row(ln_v_b))
            w_out = w_out_even_bf[p]
            wts_p = common + (w_spatial[p][:, :gl_p, :gl_p], b_spatial[p][:, :gl_p].T, w_out)
            wts_s = common + (w_spatial[p][:, :gl_s, :gl_s], b_spatial[p][:, :gl_s].T, w_out)
            hp, tail_p, _ = _even_layer(hp, bp, sp, jnp.zeros((bp, HALO_A, da), F32), wts_p, False)
            hs, tail_s, v_s = _even_layer(hs, bs, ss, _pad_rows_front(state_conv_a[p], HALO_A), wts_s, True)
            ca_p.append(tail_p)
            ca_s.append(tail_s)
            gv_s.append(v_s)
        else:
            w_in = w_in_odd[p]
            n_main = 3 * dc + rq + rkv
            w_in = jnp.concatenate([w_in, _rot_cols(w_in[:, n_main:])], axis=1).astype(BF16)
            wq = w_uq[p].reshape(rq, N_HEADS, NOPE_DIM + ROPE_DIM)
            wq = jnp.concatenate([wq, _rot_cols(wq[..., NOPE_DIM:])], axis=-1).reshape(rq, N_HEADS * HEAD_PAD)
            wq = wq.astype(BF16)
            wkv = w_ukv[p].reshape(rkv, N_HEADS, NOPE_DIM + V_DIM)
            wk = wkv[..., :NOPE_DIM].reshape(rkv, N_HEADS * NOPE_DIM).astype(BF16)
            wv = wkv[..., NOPE_DIM:].reshape(rkv, N_HEADS * V_DIM).astype(BF16)
            w_out = w_out_odd_bf[p]
            wts = (norm_mix[l], w_in, conv_c_w[p], q_norm_g[p].reshape(1, rq), wq,
                   kv_norm_g[p].reshape(1, rkv), wk, wv, w_out[:dc], w_out[dc:])
            hp, tail_p, c_p, r_p = _odd_layer(hp, bp, sp, jnp.zeros((bp, HALO_C, dc), F32), None, None,
                                              wts, cs_p)
            hs, tail_s, c_s, r_s = _odd_layer(hs, bs, ss, _pad_rows_front(state_conv_c[p], HALO_C),
                                              cache_kv_latent[p], cache_k_rope[p], wts, cs_s)
            cc_p.append(tail_p)
            cc_s.append(tail_s)
            lat_p.append(c_p)
            kr_p.append(r_p)
            lat_s.append(c_s)
            kr_s.append(r_s)
        last = l == depth - 1
        hp = _ffn(hp, norm_ffn[l], wu_all, wd_all, l, norm_final, last)
        hs = _ffn(hs, norm_ffn[l], wu_all, wd_all, l, norm_final, last)

    return (hp.reshape(bp, sp, d), hs.reshape(bs, ss, d),
            jnp.stack(ca_p), jnp.stack(ca_s), jnp.stack(gv_s), jnp.stack(cc_p), jnp.stack(cc_s),
            jnp.stack(lat_p), jnp.stack(kr_p), jnp.stack(lat_s), jnp.stack(kr_s))
```

```python
import functools
import math

import jax
import jax.numpy as jnp
from jax import lax
from jax.experimental import pallas as pl
from jax.experimental.pallas import tpu as pltpu

F32 = jnp.float32
BF16 = jnp.bfloat16

CHUNK = 64
CHUNK_SHIFT = 6
CONV_A = 31
CONV_C = 3
G_B = 8
GMLP_CHUNK = 128
N_HEADS = 16
NOPE_DIM = 128
ROPE_DIM = 64
V_DIM = 128
V_ROWS = V_DIM + 16
ROPE_THETA = 10000.0
EPS = 1e-6

LANES = 128
SUBLANES = 8
HEAD_PAD = 2 * LANES
HALO_A = 32
HALO_C = 8
MASK_VALUE = -1e30
VMEM_LIMIT = 60 * 1024 * 1024

TM_MATMUL = 1024
TN_MATMUL = 1024
TF_FFN = 1024
TS_EVEN = 256
TS_ODD = 512
TQ_ATT = 512
TK_ATT = 512
CAST_BLOCK = (1024, 2048)
KV_ROWS_STEP = 1024


def _params(*sem):
    return pltpu.CompilerParams(dimension_semantics=sem, vmem_limit_bytes=VMEM_LIMIT)


def _rms(x, g):
    return x * lax.rsqrt(jnp.mean(x * x, axis=-1, keepdims=True) + EPS) * g


def _layer_norm(x, g, b):
    xc = x - jnp.mean(x, axis=-1, keepdims=True)
    return xc * lax.rsqrt(jnp.mean(xc * xc, axis=-1, keepdims=True) + EPS) * g + b


def _sigmoid(x):
    return 0.5 * jnp.tanh(0.5 * x) + 0.5


def _gelu(x):
    return 0.5 * x * (1.0 + lax.erf(x * math.sqrt(0.5)))


def _tile(n, t):
    t = min(n, t)
    assert n % t == 0, (n, t)
    return t


def _norm_matmul_body(x_ref, g_ref, w_ref, o_ref, xn_ref):
    @pl.when(pl.program_id(1) == 0)
    def _():
        xn_ref[...] = _rms(x_ref[...], g_ref[...]).astype(BF16)

    o_ref[...] = jnp.dot(xn_ref[...], w_ref[...], preferred_element_type=F32)


def _norm_matmul(x, g, w, tn):
    t, d = x.shape
    n = w.shape[1]
    tm = _tile(t, TM_MATMUL)
    return pl.pallas_call(
        _norm_matmul_body,
        grid=(t // tm, n // tn),
        in_specs=[pl.BlockSpec((tm, d), lambda i, j: (i, 0)),
                  pl.BlockSpec((1, d), lambda i, j: (0, 0)),
                  pl.BlockSpec((d, tn), lambda i, j: (0, j))],
        out_specs=pl.BlockSpec((tm, tn), lambda i, j: (i, j)),
        out_shape=jax.ShapeDtypeStruct((t, n), F32),
        scratch_shapes=[pltpu.VMEM((tm, d), BF16)],
        compiler_params=_params("parallel", "arbitrary"),
        name="norm_matmul",
    )(x, g.reshape(1, d), w)


def _even_mid_body(*refs, ts, da, gl, want_v):
    (zc_ref, zh_ref, hist_ref, cw_ref, cb_ref, lag_ref, lab_ref, lvg_ref, lvb_ref,
     wsp_ref, bsp_ref, cat_ref, tail_ref) = refs[:13]
    v_ref = refs[13] if want_v else None
    buf_ref, conv_ref = refs[-2:]
    i = pl.program_id(1)

    buf_ref[HALO_A:HALO_A + ts, :] = zc_ref[0, :, 0:da] * _sigmoid(zc_ref[0, :, da:2 * da])

    @pl.when(i == 0)
    def _():
        buf_ref[0:HALO_A, :] = hist_ref[0]

    @pl.when(i > 0)
    def _():
        buf_ref[0:HALO_A, :] = zh_ref[0, :, 0:da] * _sigmoid(zh_ref[0, :, da:2 * da])

    tail_ref[0] = buf_ref[ts:ts + HALO_A, :]

    rc = min(ts, 128)
    base = HALO_A - (CONV_A - 1)
    win = rc + HALO_A
    for c in range(da // LANES):
        cs = slice(c * LANES, (c + 1) * LANES)
        for r0 in range(0, ts, rc):
            window = buf_ref[r0:r0 + win, cs]
            acc = jnp.broadcast_to(cb_ref[:, cs], (rc, LANES))
            for sh in range(SUBLANES):
                rolled = window if sh == 0 else pltpu.roll(window, win - sh, axis=0)
                for off in range(sh, HALO_A + 1, SUBLANES):
                    j = off - base
                    if 0 <= j < CONV_A:
                        acc = acc + cw_ref[j:j + 1, cs] * rolled[off - sh:off - sh + rc]
            conv_ref[r0:r0 + rc, cs] = acc

    y = _layer_norm(conv_ref[...], lag_ref[...], lab_ref[...])
    cat_ref[0, :, 0:da] = (y * _sigmoid(y)).astype(BF16)

    u = _gelu(zc_ref[0, :, 2 * da:3 * da])
    v = _layer_norm(_gelu(zc_ref[0, :, 3 * da:4 * da]), lvg_ref[...], lvb_ref[...])
    if want_v:
        v_ref[0] = v
    vb = v.astype(BF16)
    dh = da // G_B
    tril = lax.broadcasted_iota(jnp.int32, (gl, gl), 0) >= lax.broadcasted_iota(jnp.int32, (gl, gl), 1)
    for g in range(G_B):
        wg = jnp.where(tril, wsp_ref[g], 0.0).astype(BF16)
        gs = slice(g * dh, (g + 1) * dh)
        for k in range(ts // gl):
            rs = slice(k * gl, (k + 1) * gl)
            mixed = jnp.dot(wg, vb[rs, gs], preferred_element_type=F32) + bsp_ref[:, g:g + 1]
            cat_ref[0, rs, da + g * dh:da + (g + 1) * dh] = (u[rs, gs] * mixed).astype(BF16)


def _even_mid(z3, hist, conv_w, conv_b, lag, lab, lvg, lvb, w_sp, b_sp_t, want_v):
    b, s, n4 = z3.shape
    da = n4 // 4
    gl = w_sp.shape[-1]
    ts = _tile(s, TS_EVEN)
    assert ts % gl == 0 and ts % HALO_A == 0
    hb = ts // HALO_A
    row = lambda bi, i: (0, 0)
    out_shape = [jax.ShapeDtypeStruct((b, s, 2 * da), BF16), jax.ShapeDtypeStruct((b, HALO_A, da), F32)]
    out_specs = [pl.BlockSpec((1, ts, 2 * da), lambda bi, i: (bi, i, 0)),
                 pl.BlockSpec((1, HALO_A, da), lambda bi, i: (bi, 0, 0))]
    if want_v:
        out_shape.append(jax.ShapeDtypeStruct((b, s, da), F32))
        out_specs.append(pl.BlockSpec((1, ts, da), lambda bi, i: (bi, i, 0)))
    return pl.pallas_call(
        functools.partial(_even_mid_body, ts=ts, da=da, gl=gl, want_v=want_v),
        grid=(b, s // ts),
        in_specs=[pl.BlockSpec((1, ts, n4), lambda bi, i: (bi, i, 0)),
                  pl.BlockSpec((1, HALO_A, 2 * da), lambda bi, i: (bi, jnp.maximum(i * hb - 1, 0), 0)),
                  pl.BlockSpec((1, HALO_A, da), lambda bi, i: (bi, 0, 0)),
                  pl.BlockSpec((HALO_A, da), row),
                  pl.BlockSpec((1, da), row), pl.BlockSpec((1, da), row), pl.BlockSpec((1, da), row),
                  pl.BlockSpec((1, da), row), pl.BlockSpec((1, da), row),
                  pl.BlockSpec((G_B, gl, gl), lambda bi, i: (0, 0, 0)),
                  pl.BlockSpec((gl, G_B), row)],
        out_specs=out_specs,
        out_shape=out_shape,
        scratch_shapes=[pltpu.VMEM((HALO_A + ts, da), F32), pltpu.VMEM((ts, da), F32)],
        compiler_params=_params("parallel", "arbitrary"),
        name="even_mid",
    )(z3, z3, hist, conv_w, conv_b, lag, lab, lvg, lvb, w_sp, b_sp_t)


def _proj_residual_body(*refs, n_in):
    x_refs, w_refs = refs[:n_in], refs[n_in:2 * n_in]
    r_ref, o_ref = refs[2 * n_in], refs[2 * n_in + 1]
    acc = r_ref[...]
    for x_ref, w_ref in zip(x_refs, w_refs):
        acc = acc + jnp.dot(x_ref[...], w_ref[...], preferred_element_type=F32)
    o_ref[...] = acc


def _proj_residual(xs, ws, res):
    t, n = res.shape
    tm = _tile(t, TM_MATMUL)
    tn = _tile(n, TN_MATMUL)
    in_specs = [pl.BlockSpec((tm, x.shape[1]), lambda i, j: (i, 0)) for x in xs]
    in_specs += [pl.BlockSpec((w.shape[0], tn), lambda i, j: (0, j)) for w in ws]
    in_specs.append(pl.BlockSpec((tm, tn), lambda i, j: (i, j)))
    return pl.pallas_call(
        functools.partial(_proj_residual_body, n_in=len(xs)),
        grid=(t // tm, n // tn),
        in_specs=in_specs,
        out_specs=pl.BlockSpec((tm, tn), lambda i, j: (i, j)),
        out_shape=jax.ShapeDtypeStruct((t, n), F32),
        compiler_params=_params("parallel", "arbitrary"),
        name="proj_residual",
    )(*xs, *ws, res)


def _ffn_body(x_ref, g_ref, wu_ref, wd_ref, gf_ref, o_ref, xn_ref, *, final_norm):
    j = pl.program_id(1)

    @pl.when(j == 0)
    def _():
        x = x_ref[...]
        xn_ref[...] = _rms(x, g_ref[...]).astype(BF16)
        o_ref[...] = x

    a = jnp.maximum(jnp.dot(xn_ref[...], wu_ref[...], preferred_element_type=F32), 0.0)
    o_ref[...] += jnp.dot((a * a).astype(BF16), wd_ref[...], preferred_element_type=F32)

    if final_norm:
        @pl.when(j == pl.num_programs(1) - 1)
        def _():
            o_ref[...] = _rms(o_ref[...], gf_ref[...])


def _cast_bf16_body(w_ref, o_ref):
    o_ref[...] = w_ref[...].astype(BF16)


def _cast_bf16(w):
    nl, r, c = w.shape
    tr, tc = _tile(r, CAST_BLOCK[0]), _tile(c, CAST_BLOCK[1])
    spec = pl.BlockSpec((1, tr, tc), lambda l, i, j: (l, i, j))
    return pl.pallas_call(
        _cast_bf16_body,
        grid=(nl, r // tr, c // tc),
        in_specs=[spec],
        out_specs=spec,
        out_shape=jax.ShapeDtypeStruct(w.shape, BF16),
        compiler_params=_params("parallel", "parallel", "parallel"),
        name="cast_bf16",
    )(w)


def _ffn(x, g, wu, wd, layer, gf, final_norm):
    t, d = x.shape
    f = wu.shape[2]
    tm = _tile(t, TM_MATMUL)
    tf = _tile(f, TF_FFN)
    return pl.pallas_call(
        functools.partial(_ffn_body, final_norm=final_norm),
        grid=(t // tm, f // tf),
        in_specs=[pl.BlockSpec((tm, d), lambda i, j: (i, 0)),
                  pl.BlockSpec((1, d), lambda i, j: (0, 0)),
                  pl.BlockSpec((None, d, tf), lambda i, j: (layer, 0, j)),
                  pl.BlockSpec((None, tf, d), lambda i, j: (layer, j, 0)),
                  pl.BlockSpec((1, d), lambda i, j: (0, 0))],
        out_specs=pl.BlockSpec((tm, d), lambda i, j: (i, 0)),
        out_shape=jax.ShapeDtypeStruct((t, d), F32),
        scratch_shapes=[pltpu.VMEM((tm, d), BF16)],
        compiler_params=_params("parallel", "arbitrary"),
        name="ffn",
    )(x, g.reshape(1, d), wu, wd, gf.reshape(1, d))


def _rope_fold(y, cs):
    t = y * cs
    return t + pltpu.roll(t, ROPE_DIM, axis=1)


def _q_heads_store(wq_ref, qn, cs_t, qt_ref, scale):
    hpg = 4
    for g in range(N_HEADS // hpg):
        acc = lax.dot_general(wq_ref[g * hpg * HEAD_PAD:(g + 1) * hpg * HEAD_PAD, :], qn,
                              (((1,), (1,)), ((), ())), preferred_element_type=F32)
        for hh in range(hpg):
            h, lo = g * hpg + hh, hh * HEAD_PAD
            qt_ref[0, h, 0, 0:NOPE_DIM, :] = (acc[lo:lo + NOPE_DIM] * scale).astype(BF16)
            t = acc[lo + NOPE_DIM:lo + HEAD_PAD] * cs_t
            qt_ref[0, h, 0, NOPE_DIM:NOPE_DIM + ROPE_DIM, :] = ((t[:ROPE_DIM] + t[ROPE_DIM:]) * scale).astype(BF16)
            qt_ref[0, h, 0, NOPE_DIM + ROPE_DIM:HEAD_PAD, :] = jnp.zeros((ROPE_DIM, acc.shape[1]), BF16)


def _odd_mid_body(*refs, ts, dc, rq, rkv, q_scale):
    zc_ref, hc_ref, hx_ref, hist_ref, cw_ref, qg_ref, kvg_ref, cs_ref = refs[:8]
    if q_scale is not None:
        wq_ref, cst_ref = refs[8:10]
    cout_ref, q_out_ref, ckv_ref, kr_ref, tail_ref, buf_ref = refs[-6:]
    i = pl.program_id(1)
    gx = zc_ref[0, :, dc:2 * dc] * zc_ref[0, :, 2 * dc:3 * dc]
    buf_ref[HALO_C:HALO_C + ts, :] = gx

    @pl.when(i == 0)
    def _():
        buf_ref[0:HALO_C, :] = hist_ref[0]

    @pl.when(i > 0)
    def _():
        buf_ref[0:HALO_C, :] = hc_ref[0] * hx_ref[0]

    tail_ref[0] = buf_ref[ts:ts + HALO_C, :]
    conv = cw_ref[CONV_C - 1:CONV_C, :] * gx
    for j in range(CONV_C - 1):
        off = HALO_C - (CONV_C - 1) + j
        conv = conv + cw_ref[j:j + 1, :] * buf_ref[off:off + ts, :]
    cout_ref[0] = (zc_ref[0, :, 0:dc] * conv).astype(BF16)

    o = 3 * dc
    qn = _rms(zc_ref[0, :, o:o + rq], qg_ref[...]).astype(BF16)
    if q_scale is None:
        q_out_ref[0] = qn
    else:
        _q_heads_store(wq_ref, qn, cst_ref[...], q_out_ref, q_scale)
    ckv_ref[0] = _rms(zc_ref[0, :, o + rq:o + rq + rkv], kvg_ref[...])
    kr = _rope_fold(zc_ref[0, :, o + rq + rkv:o + rq + rkv + LANES], cs_ref[...])
    lane = lax.broadcasted_iota(jnp.int32, kr.shape, 1)
    kr_ref[0] = jnp.where(lane < ROPE_DIM, kr, 0.0)


def _odd_mid(z3, hist, conv_w, qg, kvg, cs_tab, dc, rq, rkv, wq_t=None, q_scale=None):
    b, s, n = z3.shape
    ts = _tile(s, TS_ODD)
    assert ts % HALO_C == 0 and n == 3 * dc + rq + rkv + LANES and dc % 1024 == 0
    hb = ts // HALO_C
    halo = lambda col: pl.BlockSpec((1, HALO_C, dc), lambda bi, i: (bi, jnp.maximum(i * hb - 1, 0), col))
    row = lambda bi, i: (0, 0)
    tok = lambda w: pl.BlockSpec((1, ts, w), lambda bi, i: (bi, i, 0))
    args = [z3, z3, z3, hist, conv_w, qg, kvg, cs_tab]
    in_specs = [tok(n), halo(1), halo(2),
                pl.BlockSpec((1, HALO_C, dc), lambda bi, i: (bi, 0, 0)),
                pl.BlockSpec((CONV_C, dc), row),
                pl.BlockSpec((1, rq), row), pl.BlockSpec((1, rkv), row),
                pl.BlockSpec((ts, LANES), lambda bi, i: (i, 0))]
    if wq_t is None:
        q_spec, q_shape = tok(rq), jax.ShapeDtypeStruct((b, s, rq), BF16)
    else:
        args += [wq_t, cs_tab.T]
        in_specs += [pl.BlockSpec(wq_t.shape, row), pl.BlockSpec((LANES, ts), lambda bi, i: (0, i))]
        q_spec = pl.BlockSpec((1, N_HEADS, 1, HEAD_PAD, ts), lambda bi, i: (bi, 0, i, 0, 0))
        q_shape = jax.ShapeDtypeStruct((b, N_HEADS, s // ts, HEAD_PAD, ts), BF16)
    return pl.pallas_call(
        functools.partial(_odd_mid_body, ts=ts, dc=dc, rq=rq, rkv=rkv, q_scale=q_scale),
        grid=(b, s // ts),
        in_specs=in_specs,
        out_specs=[tok(dc), q_spec, tok(rkv), tok(LANES),
                   pl.BlockSpec((1, HALO_C, dc), lambda bi, i: (bi, 0, 0))],
        out_shape=[jax.ShapeDtypeStruct((b, s, dc), BF16),
                   q_shape,
                   jax.ShapeDtypeStruct((b, s, rkv), F32),
                   jax.ShapeDtypeStruct((b, s, LANES), F32),
                   jax.ShapeDtypeStruct((b, HALO_C, dc), F32)],
        scratch_shapes=[pltpu.VMEM((HALO_C + ts, dc), F32)],
        compiler_params=_params("parallel", "arbitrary"),
        name="odd_mid",
    )(*args)


def _kv_up_body(c_ref, kr_ref, wk_ref, wvt_ref, k_ref, vt_ref):
    c = c_ref[...].astype(BF16)
    k = jnp.dot(c, wk_ref[...], preferred_element_type=F32)
    vt = lax.dot_general(wvt_ref[...], c, (((1,), (1,)), ((), ())), preferred_element_type=F32)
    kr = kr_ref[...].astype(BF16)
    tk = vt_ref.shape[-1]
    for h in range(k.shape[1] // NOPE_DIM):
        k_ref[:, h * HEAD_PAD:h * HEAD_PAD + NOPE_DIM] = k[:, h * NOPE_DIM:(h + 1) * NOPE_DIM].astype(BF16)
        k_ref[:, h * HEAD_PAD + NOPE_DIM:(h + 1) * HEAD_PAD] = kr
        for t in range(vt_ref.shape[2]):
            vt_ref[0, h, t, 0:V_DIM, :] = vt[h * V_DIM:(h + 1) * V_DIM, t * tk:(t + 1) * tk].astype(BF16)
            vt_ref[0, h, t, V_DIM:V_ROWS, :] = jnp.ones((V_ROWS - V_DIM, tk), BF16)


def _kv_up(c, kr, wk, wv_t, b, skv, tk):
    t, r = c.shape
    nkt = skv // tk
    tm = _tile(skv, KV_ROWS_STEP)
    kts = tm // tk
    per = skv // tm
    hpt = 4
    return pl.pallas_call(
        _kv_up_body,
        grid=(t // tm, N_HEADS // hpt),
        in_specs=[pl.BlockSpec((tm, r), lambda i, j: (i, 0)),
                  pl.BlockSpec((tm, LANES), lambda i, j: (i, 0)),
                  pl.BlockSpec((r, hpt * NOPE_DIM), lambda i, j: (0, j)),
                  pl.BlockSpec((hpt * V_DIM, r), lambda i, j: (j, 0))],
        out_specs=[pl.BlockSpec((tm, hpt * HEAD_PAD), lambda i, j: (i, j)),
                   pl.BlockSpec((1, hpt, kts, V_ROWS, tk), lambda i, j: (i // per, j, i % per, 0, 0))],
        out_shape=[jax.ShapeDtypeStruct((t, N_HEADS * HEAD_PAD), BF16),
                   jax.ShapeDtypeStruct((b, N_HEADS, nkt, V_ROWS, tk), BF16)],
        compiler_params=_params("parallel", "arbitrary"),
        name="kv_up",
    )(c, kr, wk, wv_t)


def _colmax(s):
    while s.shape[0] > 8 and s.shape[0] % 16 == 0:
        half = s.shape[0] // 2
        s = jnp.maximum(s[:half], s[half:])
    return jnp.max(s, axis=0, keepdims=True)


def _attention_body(qt_ref, k_ref, vt_ref, o_ref, s_ref, p_ref, acc_ref, *, tq, tk, nkt, past, t_valid):
    def k_tile(kt):
        return k_ref[0, kt * tk:(kt + 1) * tk, :]

    def half_step(kt, n_steps, carry, qt, q0, masked):
        m, alpha = carry
        slot, other = kt % 2, 1 - kt % 2
        if kt >= 1:
            pv = jnp.dot(vt_ref[0, 0, kt - 1], p_ref[other], preferred_element_type=F32)
        if kt + 1 < n_steps:
            s_next = jnp.dot(k_tile(kt + 1), qt, preferred_element_type=F32)
        s = s_ref[slot]
        if masked:
            k_pos = kt * tk + lax.broadcasted_iota(jnp.int32, (tk, 1), 0)
            q_pos = q0 + lax.broadcasted_iota(jnp.int32, (1, tq), 1)
            k_chunk = jnp.where(k_pos < t_valid, k_pos >> CHUNK_SHIFT, jnp.iinfo(jnp.int32).max)
            s = jnp.where(k_chunk <= q_pos >> CHUNK_SHIFT, s, MASK_VALUE)
        m_new = jnp.maximum(m, _colmax(s))
        p_ref[slot] = jnp.exp2(s - m_new).astype(BF16)
        if kt == 1:
            acc_ref[...] = pv
        elif kt > 1:
            acc_ref[...] = alpha * acc_ref[...] + pv
        if kt + 1 < n_steps:
            s_ref[other] = s_next
        return m_new, jnp.exp2(m - m_new)

    for qi in range(qt_ref.shape[2]):
        qt = qt_ref[0, 0, qi]
        q0 = past + qi * tq
        n_full = min(((q0 >> CHUNK_SHIFT) + 1) * CHUNK, t_valid) // tk
        n_all = -(-min((((q0 + tq - 1) >> CHUNK_SHIFT) + 1) * CHUNK, t_valid) // tk)
        s_ref[0] = jnp.dot(k_tile(0), qt, preferred_element_type=F32)
        carry = (jnp.full((1, tq), MASK_VALUE, F32), None)
        for kt in range(n_all):
            carry = half_step(kt, n_all, carry, qt=qt, q0=q0, masked=kt >= n_full)
        acc = jnp.dot(vt_ref[0, 0, n_all - 1], p_ref[(n_all - 1) % 2], preferred_element_type=F32)
        if n_all >= 2:
            acc = carry[1] * acc_ref[...] + acc
        o_ref[0, qi * tq:(qi + 1) * tq, :] = jnp.transpose(acc[:V_DIM] / acc[V_DIM:V_DIM + 1]).astype(BF16)


def _attention(qt, k3, vt, past, t_valid):
    b, _, nq, _, tq = qt.shape
    skv = k3.shape[1]
    nkt, tk = vt.shape[2], vt.shape[4]
    return pl.pallas_call(
        functools.partial(_attention_body, tq=tq, tk=tk, nkt=nkt, past=past, t_valid=t_valid),
        scratch_shapes=[pltpu.VMEM((2, tk, tq), F32), pltpu.VMEM((2, tk, tq), BF16), pltpu.VMEM((V_ROWS, tq), F32)],
        grid=(b, N_HEADS),
        in_specs=[pl.BlockSpec((1, 1, nq, HEAD_PAD, tq), lambda bi, h: (bi, h, 0, 0, 0)),
                  pl.BlockSpec((1, skv, HEAD_PAD), lambda bi, h: (bi, 0, h)),
                  pl.BlockSpec((1, 1, nkt, V_ROWS, tk), lambda bi, h: (bi, h, 0, 0, 0))],
        out_specs=pl.BlockSpec((1, nq * tq, V_DIM), lambda bi, h: (bi, 0, h)),
        out_shape=jax.ShapeDtypeStruct((b, nq * tq, N_HEADS * V_DIM), BF16),
        compiler_params=_params("parallel", "parallel"),
        name="attention",
    )(qt, k3, vt)


def _latent_attention_body(qn_ref, wq_ref, wuk_ref, cs_ref, lat_ref, krp_ref, cnew_ref, krnew_ref, wuv_ref,
                           o_ref, q_ref, acc_ref, *, ss, tk, n_cache, past, scale, rkv):
    qn = qn_ref[0]
    cs = cs_ref[...]
    for h in range(N_HEADS):
        qh = jnp.dot(qn, wq_ref[:, h * HEAD_PAD:(h + 1) * HEAD_PAD], preferred_element_type=F32)
        q_abs = lax.dot_general(qh[:, :NOPE_DIM].astype(BF16), wuk_ref[:, h * NOPE_DIM:(h + 1) * NOPE_DIM],
                                (((1,), (1,)), ((), ())), preferred_element_type=F32)
        q_ref[h * ss:(h + 1) * ss, 0:rkv] = (q_abs * scale).astype(BF16)
        q_ref[h * ss:(h + 1) * ss, rkv:rkv + LANES] = (_rope_fold(qh[:, NOPE_DIM:], cs) * scale).astype(BF16)

    q_lat = q_ref[:, 0:rkv]
    q_rope = q_ref[:, rkv:rkv + ROPE_DIM]
    cols = N_HEADS * ss
    q_chunk = (past + lax.broadcasted_iota(jnp.int32, (1, cols), 1) % ss) >> CHUNK_SHIFT
    acc_ref[...] = jnp.zeros((rkv, cols), F32)
    nt = (((1,), (1,)), ((), ()))

    def update(c, kr, k0, carry):
        m, l = carry
        c = c.astype(BF16)
        s = (lax.dot_general(c, q_lat, nt, preferred_element_type=F32)
             + lax.dot_general(kr.astype(BF16), q_rope, nt, preferred_element_type=F32))
        k_chunk = (k0 + lax.broadcasted_iota(jnp.int32, (c.shape[0], 1), 0)) >> CHUNK_SHIFT
        s = jnp.where(k_chunk <= q_chunk, s, MASK_VALUE)
        m_new = jnp.maximum(m, _colmax(s))
        alpha = jnp.exp2(m - m_new)
        p = jnp.exp2(s - m_new)
        pv = lax.dot_general(c, p.astype(BF16), (((0,), (0,)), ((), ())), preferred_element_type=F32)
        acc_ref[...] = alpha * acc_ref[...] + pv
        return m_new, alpha * l + jnp.sum(p, axis=0, keepdims=True)

    def cache_step(kt, carry):
        rows = pl.ds(pl.multiple_of(kt * tk, tk), tk)
        return update(lat_ref[0, rows, :], krp_ref[0, rows, :], kt * tk, carry)

    carry = (jnp.full((1, cols), MASK_VALUE, F32), jnp.zeros((1, cols), F32))
    carry = lax.fori_loop(0, n_cache, cache_step, carry)
    _, l = update(cnew_ref[0], krnew_ref[0, :, 0:ROPE_DIM], past, carry)
    o_lat = jnp.transpose(acc_ref[...] / l).astype(BF16)
    for h in range(N_HEADS):
        o_ref[0, :, h * V_DIM:(h + 1) * V_DIM] = jnp.dot(
            o_lat[h * ss:(h + 1) * ss], wuv_ref[:, h * V_DIM:(h + 1) * V_DIM],
            preferred_element_type=F32).astype(BF16)


def _latent_attention(qn, wq, wuk, cs_tab, lat_prev, kr_prev, c_new, kr_new, wuv, scale):
    b, ss, rq = qn.shape
    past, rkv = lat_prev.shape[1], lat_prev.shape[2]
    tk = min(TK_ATT, past)
    see_cache = min((((past + ss - 1) >> CHUNK_SHIFT) + 1) * CHUNK, past)
    n_cache = -(-see_cache // tk)
    assert past % tk == 0
    whole = lambda a: pl.BlockSpec(a.shape, lambda bi: (0,) * a.ndim)
    per_b = lambda a: pl.BlockSpec((1,) + a.shape[1:], lambda bi: (bi, 0, 0))
    return pl.pallas_call(
        functools.partial(_latent_attention_body, ss=ss, tk=tk, n_cache=n_cache, past=past, scale=scale, rkv=rkv),
        grid=(b,),
        in_specs=[per_b(qn), whole(wq), whole(wuk), whole(cs_tab),
                  per_b(lat_prev), per_b(kr_prev), per_b(c_new), per_b(kr_new), whole(wuv)],
        out_specs=pl.BlockSpec((1, ss, N_HEADS * V_DIM), lambda bi: (bi, 0, 0)),
        out_shape=jax.ShapeDtypeStruct((b, ss, N_HEADS * V_DIM), BF16),
        scratch_shapes=[pltpu.VMEM((N_HEADS * ss, rkv + LANES), BF16), pltpu.VMEM((rkv, N_HEADS * ss), F32)],
        compiler_params=_params("parallel"),
        name="latent_attention",
    )(qn, wq, wuk, cs_tab, lat_prev, kr_prev, c_new, kr_new, wuv)


def _rope_table(past, s):
    half = ROPE_DIM // 2
    inv = ROPE_THETA ** (-jnp.arange(half, dtype=F32) / half)
    ang = (past + jnp.arange(s, dtype=jnp.int32)).astype(F32)[:, None] * inv[None, :]
    cos, sin = jnp.cos(ang), jnp.sin(ang)
    return jnp.concatenate([cos, cos, sin, sin], axis=-1)


def _rot_cols(w):
    half = ROPE_DIM // 2
    return jnp.concatenate([-w[..., half:], w[..., :half]], axis=-1)


def _pad_rows_front(x, rows):
    return jnp.pad(x, ((0, 0), (rows - x.shape[1], 0), (0, 0)))


def _even_layer(h, b, s, hist, wts, want_v):
    (g_mix, w_in, conv_w, conv_b, lag, lab, lvg, lvb, w_sp, b_sp_t, w_out) = wts
    z = _norm_matmul(h, g_mix, w_in, _tile(w_in.shape[1], TN_MATMUL))
    outs = _even_mid(z.reshape(b, s, -1), hist, conv_w, conv_b, lag, lab, lvg, lvb, w_sp, b_sp_t, want_v)
    cat, tail = outs[0], outs[1]
    h = _proj_residual([cat.reshape(b * s, -1)], [w_out], h)
    return h, tail[:, HALO_A - (CONV_A - 1):], (outs[2] if want_v else None)


def _odd_layer(h, b, s, hist, lat_prev, kr_prev, wts, cs_tab):
    (g_mix, w_in, conv_w, qg, wq, kvg, w_k, w_v, w_out_c, w_out_a) = wts
    dc = conv_w.shape[1]
    rq, rkv = qg.shape[1], kvg.shape[1]
    z = _norm_matmul(h, g_mix, w_in, w_in.shape[1] // 3)
    scale = math.log2(math.e) / math.sqrt(NOPE_DIM + ROPE_DIM)
    if lat_prev is None:
        assert s % LANES == 0 and TS_ODD == TQ_ATT
        cout, qt, ckv, kr, tail = _odd_mid(z.reshape(b, s, -1), hist, conv_w, qg, kvg, cs_tab, dc, rq, rkv,
                                           wq.T, scale)
        tk = min(TK_ATT, s)
        k, vt = _kv_up(ckv.reshape(b * s, rkv), kr.reshape(b * s, LANES), w_k, w_v.T, b, s, tk)
        attn = _attention(qt, k.reshape(b, s, -1), vt, 0, s)
    else:
        cout, qn, ckv, kr, tail = _odd_mid(z.reshape(b, s, -1), hist, conv_w, qg, kvg, cs_tab, dc, rq, rkv)
        attn = _latent_attention(qn, wq, w_k, cs_tab, lat_prev, kr_prev, ckv, kr, w_v, scale)
    h = _proj_residual([cout.reshape(b * s, dc), attn.reshape(b * s, -1)], [w_out_c, w_out_a], h)
    return h, tail[:, HALO_C - (CONV_C - 1):], ckv, kr[..., :ROPE_DIM]


def kernel(x_prompt, x_sample, state_conv_a, state_conv_c, cache_kv_latent, cache_k_rope, norm_mix, norm_ffn, norm_final, w_in_even, conv_a_w, conv_a_b, ln_a_g, ln_a_b, ln_v_g, ln_v_b, w_spatial, b_spatial, w_out_even, w_in_odd, conv_c_w, q_norm_g, w_uq, kv_norm_g, w_ukv, w_out_odd, w_ffn_up, w_ffn_down):
    bp, sp, d = x_prompt.shape
    bs, ss, _ = x_sample.shape
    depth = norm_mix.shape[0]
    da = conv_a_w.shape[-1]
    dc = conv_c_w.shape[-1]
    rq, rkv = q_norm_g.shape[-1], kv_norm_g.shape[-1]
    past = cache_kv_latent.shape[2]
    assert sp % GMLP_CHUNK == 0 and ss <= GMLP_CHUNK and ss >= HALO_A and past % GMLP_CHUNK == 0

    hp = x_prompt.reshape(bp * sp, d)
    hs = x_sample.reshape(bs * ss, d)
    cs_p = _rope_table(0, sp)
    cs_s = _rope_table(past, ss)
    wu_all, wd_all = _cast_bf16(w_ffn_up), _cast_bf16(w_ffn_down)
    w_in_even_bf, w_out_even_bf, w_out_odd_bf = _cast_bf16(w_in_even), _cast_bf16(w_out_even), _cast_bf16(w_out_odd)

    ca_p, ca_s, gv_s, cc_p, cc_s, lat_p, kr_p, lat_s, kr_s = [], [], [], [], [], [], [], [], []
    for l in range(depth):
        p = l // 2
        if l % 2 == 0:
            row = lambda a: a[p].reshape(1, -1)
            gl_p, gl_s = min(sp, GMLP_CHUNK), min(ss, GMLP_CHUNK)
            common = (norm_mix[l], w_in_even_bf[p],
                      jnp.pad(conv_a_w[p], ((0, HALO_A - CONV_A), (0, 0))), row(conv_a_b),
                      row(ln_a_g), row(ln_a_b), row(ln_v_g), row(ln_v_b))
            w_out = w_out_even_bf[p]
            wts_p = common + (w_spatial[p][:, :gl_p, :gl_p], b_spatial[p][:, :gl_p].T, w_out)
            wts_s = common + (w_spatial[p][:, :gl_s, :gl_s], b_spatial[p][:, :gl_s].T, w_out)
            hp, tail_p, _ = _even_layer(hp, bp, sp, jnp.zeros((bp, HALO_A, da), F32), wts_p, False)
            hs, tail_s, v_s = _even_layer(hs, bs, ss, _pad_rows_front(state_conv_a[p], HALO_A), wts_s, True)
            ca_p.append(tail_p)
            ca_s.append(tail_s)
            gv_s.append(v_s)
        else:
            w_in = w_in_odd[p]
            n_main = 3 * dc + rq + rkv
            w_in = jnp.concatenate([w_in, _rot_cols(w_in[:, n_main:])], axis=1).astype(BF16)
            wq = w_uq[p].reshape(rq, N_HEADS, NOPE_DIM + ROPE_DIM)
            wq = jnp.concatenate([wq, _rot_cols(wq[..., NOPE_DIM:])], axis=-1).reshape(rq, N_HEADS * HEAD_PAD)
            wq = wq.astype(BF16)
            wkv = w_ukv[p].reshape(rkv, N_HEADS, NOPE_DIM + V_DIM)
            wk = wkv[..., :NOPE_DIM].reshape(rkv, N_HEADS * NOPE_DIM).astype(BF16)
            wv = wkv[..., NOPE_DIM:].reshape(rkv, N_HEADS * V_DIM).astype(BF16)
            w_out = w_out_odd_bf[p]
            wts = (norm_mix[l], w_in, conv_c_w[p], q_norm_g[p].reshape(1, rq), wq,
                   kv_norm_g[p].reshape(1, rkv), wk, wv, w_out[:dc], w_out[dc:])
            hp, tail_p, c_p, r_p = _odd_layer(hp, bp, sp, jnp.zeros((bp, HALO_C, dc), F32), None, None,
                                              wts, cs_p)
            hs, tail_s, c_s, r_s = _odd_layer(hs, bs, ss, _pad_rows_front(state_conv_c[p], HALO_C),
                                              cache_kv_latent[p], cache_k_rope[p], wts, cs_s)
            cc_p.append(tail_p)
            cc_s.append(tail_s)
            lat_p.append(c_p)
            kr_p.append(r_p)
            lat_s.append(c_s)
            kr_s.append(r_s)
        last = l == depth - 1
        hp = _ffn(hp, norm_ffn[l], wu_all, wd_all, l, norm_final, last)
        hs = _ffn(hs, norm_ffn[l], wu_all, wd_all, l, norm_final, last)

    return (hp.reshape(bp, sp, d), hs.reshape(bs, ss, d),
            jnp.stack(ca_p), jnp.stack(ca_s), jnp.stack(gv_s), jnp.stack(cc_p), jnp.stack(cc_s),
            jnp.stack(lat_p), jnp.stack(kr_p), jnp.stack(lat_s), jnp.stack(kr_s))
```

```python
import functools
import math

import jax
import jax.numpy as jnp
from jax import lax
from jax.experimental import pallas as pl
from jax.experimental.pallas import tpu as pltpu

F32 = jnp.float32
BF16 = jnp.bfloat16

CHUNK = 64
CHUNK_SHIFT = 6
CONV_A = 31
CONV_C = 3
G_B = 8
GMLP_CHUNK = 128
N_HEADS = 16
NOPE_DIM = 128
ROPE_DIM = 64
V_DIM = 128
V_ROWS = V_DIM + 16
ROPE_THETA = 10000.0
EPS = 1e-6

LANES = 128
SUBLANES = 8
HEAD_PAD = 2 * LANES
HALO_A = 32
HALO_C = 8
CONV_A_ROWS = -(-CONV_A // SUBLANES) * SUBLANES
MASK_VALUE = -1e30
VMEM_LIMIT = 60 * 1024 * 1024

TM_MATMUL = 1024
TN_MATMUL = 1024
TF_FFN = 1024
TS_EVEN = 256
TS_ODD = 512
TQ_ATT = 512
TK_ATT = 512
CAST_BLOCK = (1024, 2048)
KV_ROWS_STEP = 1024
TN_IN_EVEN = 2048


def _params(*sem):
    return pltpu.CompilerParams(dimension_semantics=sem, vmem_limit_bytes=VMEM_LIMIT)


def _rms(x, g):
    return x * lax.rsqrt(jnp.mean(x * x, axis=-1, keepdims=True) + EPS) * g


def _layer_norm(x, g, b):
    xc = x - jnp.mean(x, axis=-1, keepdims=True)
    return xc * lax.rsqrt(jnp.mean(xc * xc, axis=-1, keepdims=True) + EPS) * g + b


def _sigmoid(x):
    return 0.5 * jnp.tanh(0.5 * x) + 0.5


def _gelu(x):
    return 0.5 * x * (1.0 + lax.erf(x * math.sqrt(0.5)))


def _tile(n, t):
    t = min(n, t)
    assert n % t == 0, (n, t)
    return t


def _norm_matmul_body(x_ref, g_ref, w_ref, o_ref, xn_ref):
    @pl.when(pl.program_id(1) == 0)
    def _():
        xn_ref[...] = _rms(x_ref[...], g_ref[...]).astype(BF16)

    o_ref[...] = jnp.dot(xn_ref[...], w_ref[...], preferred_element_type=F32)


def _norm_matmul(x, g, w, tn):
    t, d = x.shape
    n = w.shape[1]
    tm = _tile(t, TM_MATMUL)
    return pl.pallas_call(
        _norm_matmul_body,
        grid=(t // tm, n // tn),
        in_specs=[pl.BlockSpec((tm, d), lambda i, j: (i, 0)),
                  pl.BlockSpec((1, d), lambda i, j: (0, 0)),
                  pl.BlockSpec((d, tn), lambda i, j: (0, j))],
        out_specs=pl.BlockSpec((tm, tn), lambda i, j: (i, j)),
        out_shape=jax.ShapeDtypeStruct((t, n), F32),
        scratch_shapes=[pltpu.VMEM((tm, d), BF16)],
        compiler_params=_params("parallel", "arbitrary"),
        name="norm_matmul",
    )(x, g.reshape(1, d), w)


def _even_mid_body(*refs, ts, da, gl, want_v):
    (zc_ref, zh_ref, hist_ref, cw_ref, cb_ref, lag_ref, lab_ref, lvg_ref, lvb_ref,
     wsp_ref, bsp_ref, cat_ref, tail_ref) = refs[:13]
    v_ref = refs[13] if want_v else None
    buf_ref, conv_ref = refs[-2:]
    i = pl.program_id(1)

    buf_ref[HALO_A:HALO_A + ts, :] = zc_ref[0, :, 0:da] * _sigmoid(zc_ref[0, :, da:2 * da])

    @pl.when(i == 0)
    def _():
        buf_ref[0:HALO_A, :] = hist_ref[0]

    @pl.when(i > 0)
    def _():
        buf_ref[0:HALO_A, :] = zh_ref[0, :, 0:da] * _sigmoid(zh_ref[0, :, da:2 * da])

    tail_ref[0] = buf_ref[ts:ts + HALO_A, :]

    rc = min(ts, 128)
    base = HALO_A - (CONV_A - 1)
    win = rc + HALO_A
    for c in range(da // LANES):
        cs = slice(c * LANES, (c + 1) * LANES)
        for r0 in range(0, ts, rc):
            window = buf_ref[r0:r0 + win, cs]
            acc = jnp.broadcast_to(cb_ref[:, cs], (rc, LANES))
            for sh in range(SUBLANES):
                rolled = window if sh == 0 else pltpu.roll(window, win - sh, axis=0)
                for off in range(sh, HALO_A + 1, SUBLANES):
                    j = off - base
                    if 0 <= j < CONV_A:
                        acc = acc + cw_ref[j:j + 1, cs] * rolled[off - sh:off - sh + rc]
            conv_ref[r0:r0 + rc, cs] = acc

    y = _layer_norm(conv_ref[...], lag_ref[...], lab_ref[...])
    cat_ref[0, :, 0:da] = (y * _sigmoid(y)).astype(BF16)

    u = _gelu(zc_ref[0, :, 2 * da:3 * da])
    v = _layer_norm(_gelu(zc_ref[0, :, 3 * da:4 * da]), lvg_ref[...], lvb_ref[...])
    if want_v:
        v_ref[0] = v
    vb = v.astype(BF16)
    dh = da // G_B
    tril = lax.broadcasted_iota(jnp.int32, (gl, gl), 0) >= lax.broadcasted_iota(jnp.int32, (gl, gl), 1)
    for g in range(G_B):
        wg = jnp.where(tril, wsp_ref[g], 0.0).astype(BF16)
        gs = slice(g * dh, (g + 1) * dh)
        for k in range(ts // gl):
            rs = slice(k * gl, (k + 1) * gl)
            mixed = jnp.dot(wg, vb[rs, gs], preferred_element_type=F32) + bsp_ref[:, g:g + 1]
            cat_ref[0, rs, da + g * dh:da + (g + 1) * dh] = (u[rs, gs] * mixed).astype(BF16)


def _even_mid(z3, hist, conv_w, conv_b, lag, lab, lvg, lvb, w_sp, b_sp_t, want_v):
    b, s, n4 = z3.shape
    da = n4 // 4
    gl = w_sp.shape[-1]
    ts = _tile(s, TS_EVEN)
    assert ts % gl == 0 and ts % HALO_A == 0
    hb = ts // HALO_A
    row = lambda bi, i: (0, 0)
    out_shape = [jax.ShapeDtypeStruct((b, s, 2 * da), BF16), jax.ShapeDtypeStruct((b, HALO_A, da), F32)]
    out_specs = [pl.BlockSpec((1, ts, 2 * da), lambda bi, i: (bi, i, 0)),
                 pl.BlockSpec((1, HALO_A, da), lambda bi, i: (bi, 0, 0))]
    if want_v:
        out_shape.append(jax.ShapeDtypeStruct((b, s, da), F32))
        out_specs.append(pl.BlockSpec((1, ts, da), lambda bi, i: (bi, i, 0)))
    return pl.pallas_call(
        functools.partial(_even_mid_body, ts=ts, da=da, gl=gl, want_v=want_v),
        grid=(b, s // ts),
        in_specs=[pl.BlockSpec((1, ts, n4), lambda bi, i: (bi, i, 0)),
                  pl.BlockSpec((1, HALO_A, 2 * da), lambda bi, i: (bi, jnp.maximum(i * hb - 1, 0), 0)),
                  pl.BlockSpec((1, HALO_A, da), lambda bi, i: (bi, 0, 0)),
                  pl.BlockSpec((CONV_A_ROWS, da), row),
                  pl.BlockSpec((1, da), row), pl.BlockSpec((1, da), row), pl.BlockSpec((1, da), row),
                  pl.BlockSpec((1, da), row), pl.BlockSpec((1, da), row),
                  pl.BlockSpec((G_B, gl, gl), lambda bi, i: (0, 0, 0)),
                  pl.BlockSpec((gl, G_B), row)],
        out_specs=out_specs,
        out_shape=out_shape,
        scratch_shapes=[pltpu.VMEM((HALO_A + ts, da), F32), pltpu.VMEM((ts, da), F32)],
        compiler_params=_params("parallel", "arbitrary"),
        name="even_mid",
    )(z3, z3, hist, conv_w, conv_b, lag, lab, lvg, lvb, w_sp, b_sp_t)


def _proj_residual_body(*refs, n_in):
    x_refs, w_refs = refs[:n_in], refs[n_in:2 * n_in]
    r_ref, o_ref = refs[2 * n_in], refs[2 * n_in + 1]
    acc = r_ref[...]
    for x_ref, w_ref in zip(x_refs, w_refs):
        acc = acc + jnp.dot(x_ref[...], w_ref[...], preferred_element_type=F32)
    o_ref[...] = acc


def _proj_residual(xs, ws, res):
    t, n = res.shape
    tm = _tile(t, TM_MATMUL)
    tn = _tile(n, TN_MATMUL)
    in_specs = [pl.BlockSpec((tm, x.shape[1]), lambda i, j: (i, 0)) for x in xs]
    in_specs += [pl.BlockSpec((w.shape[0], tn), lambda i, j: (0, j)) for w in ws]
    in_specs.append(pl.BlockSpec((tm, tn), lambda i, j: (i, j)))
    return pl.pallas_call(
        functools.partial(_proj_residual_body, n_in=len(xs)),
        grid=(t // tm, n // tn),
        in_specs=in_specs,
        out_specs=pl.BlockSpec((tm, tn), lambda i, j: (i, j)),
        out_shape=jax.ShapeDtypeStruct((t, n), F32),
        compiler_params=_params("parallel", "arbitrary"),
        name="proj_residual",
    )(*xs, *ws, res)


def _ffn_body(x_ref, g_ref, wu_ref, wd_ref, gf_ref, o_ref, xn_ref, *, final_norm):
    j = pl.program_id(1)

    @pl.when(j == 0)
    def _():
        x = x_ref[...]
        xn_ref[...] = _rms(x, g_ref[...]).astype(BF16)
        o_ref[...] = x

    a = jnp.maximum(jnp.dot(xn_ref[...], wu_ref[...], preferred_element_type=F32), 0.0)
    o_ref[...] += jnp.dot((a * a).astype(BF16), wd_ref[...], preferred_element_type=F32)

    if final_norm:
        @pl.when(j == pl.num_programs(1) - 1)
        def _():
            o_ref[...] = _rms(o_ref[...], gf_ref[...])


def _cast_bf16_body(w_ref, o_ref):
    o_ref[...] = w_ref[...].astype(BF16)


def _cast_bf16(w):
    nl, r, c = w.shape
    tr, tc = _tile(r, CAST_BLOCK[0]), _tile(c, CAST_BLOCK[1])
    spec = pl.BlockSpec((1, tr, tc), lambda l, i, j: (l, i, j))
    return pl.pallas_call(
        _cast_bf16_body,
        grid=(nl, r // tr, c // tc),
        in_specs=[spec],
        out_specs=spec,
        out_shape=jax.ShapeDtypeStruct(w.shape, BF16),
        compiler_params=_params("parallel", "parallel", "parallel"),
        name="cast_bf16",
    )(w)


def _ffn(x, g, wu, wd, layer, gf, final_norm):
    t, d = x.shape
    f = wu.shape[2]
    tm = _tile(t, TM_MATMUL)
    tf = _tile(f, TF_FFN)
    return pl.pallas_call(
        functools.partial(_ffn_body, final_norm=final_norm),
        grid=(t // tm, f // tf),
        in_specs=[pl.BlockSpec((tm, d), lambda i, j: (i, 0)),
                  pl.BlockSpec((1, d), lambda i, j: (0, 0)),
                  pl.BlockSpec((None, d, tf), lambda i, j: (layer, 0, j)),
                  pl.BlockSpec((None, tf, d), lambda i, j: (layer, j, 0)),
                  pl.BlockSpec((1, d), lambda i, j: (0, 0))],
        out_specs=pl.BlockSpec((tm, d), lambda i, j: (i, 0)),
        out_shape=jax.ShapeDtypeStruct((t, d), F32),
        scratch_shapes=[pltpu.VMEM((tm, d), BF16)],
        compiler_params=_params("parallel", "arbitrary"),
        name="ffn",
    )(x, g.reshape(1, d), wu, wd, gf.reshape(1, d))


def _rope_fold(y, cs):
    t = y * cs
    return t + pltpu.roll(t, ROPE_DIM, axis=1)


def _q_heads_store(wq_ref, qn, cs_t, qt_ref, scale):
    hpg = 4
    for g in range(N_HEADS // hpg):
        acc = lax.dot_general(wq_ref[g * hpg * HEAD_PAD:(g + 1) * hpg * HEAD_PAD, :], qn,
                              (((1,), (1,)), ((), ())), preferred_element_type=F32)
        for hh in range(hpg):
            h, lo = g * hpg + hh, hh * HEAD_PAD
            qt_ref[0, h, 0, 0:NOPE_DIM, :] = (acc[lo:lo + NOPE_DIM] * scale).astype(BF16)
            t = acc[lo + NOPE_DIM:lo + HEAD_PAD] * cs_t
            qt_ref[0, h, 0, NOPE_DIM:NOPE_DIM + ROPE_DIM, :] = ((t[:ROPE_DIM] + t[ROPE_DIM:]) * scale).astype(BF16)
            qt_ref[0, h, 0, NOPE_DIM + ROPE_DIM:HEAD_PAD, :] = jnp.zeros((ROPE_DIM, acc.shape[1]), BF16)


def _odd_mid_body(*refs, ts, dc, rq, rkv, q_scale):
    zc_ref, hc_ref, hx_ref, hist_ref, cw_ref, qg_ref, kvg_ref, cs_ref = refs[:8]
    if q_scale is not None:
        wq_ref, cst_ref = refs[8:10]
    cout_ref, q_out_ref, ckv_ref, kr_ref, tail_ref, buf_ref = refs[-6:]
    i = pl.program_id(1)
    gx = zc_ref[0, :, dc:2 * dc] * zc_ref[0, :, 2 * dc:3 * dc]
    buf_ref[HALO_C:HALO_C + ts, :] = gx

    @pl.when(i == 0)
    def _():
        buf_ref[0:HALO_C, :] = hist_ref[0]

    @pl.when(i > 0)
    def _():
        buf_ref[0:HALO_C, :] = hc_ref[0] * hx_ref[0]

    tail_ref[0] = buf_ref[ts:ts + HALO_C, :]
    conv = cw_ref[CONV_C - 1:CONV_C, :] * gx
    for j in range(CONV_C - 1):
        off = HALO_C - (CONV_C - 1) + j
        conv = conv + cw_ref[j:j + 1, :] * buf_ref[off:off + ts, :]
    cout_ref[0] = (zc_ref[0, :, 0:dc] * conv).astype(BF16)

    o = 3 * dc
    qn = _rms(zc_ref[0, :, o:o + rq], qg_ref[...]).astype(BF16)
    if q_scale is None:
        q_out_ref[0] = qn
    else:
        _q_heads_store(wq_ref, qn, cst_ref[...], q_out_ref, q_scale)
    ckv_ref[0] = _rms(zc_ref[0, :, o + rq:o + rq + rkv], kvg_ref[...])
    kr = _rope_fold(zc_ref[0, :, o + rq + rkv:o + rq + rkv + LANES], cs_ref[...])
    lane = lax.broadcasted_iota(jnp.int32, kr.shape, 1)
    kr_ref[0] = jnp.where(lane < ROPE_DIM, kr, 0.0)


def _odd_mid(z3, hist, conv_w, qg, kvg, cs_tab, dc, rq, rkv, wq_t=None, q_scale=None):
    b, s, n = z3.shape
    ts = _tile(s, TS_ODD)
    assert ts % HALO_C == 0 and n == 3 * dc + rq + rkv + LANES and dc % 1024 == 0
    hb = ts // HALO_C
    halo = lambda col: pl.BlockSpec((1, HALO_C, dc), lambda bi, i: (bi, jnp.maximum(i * hb - 1, 0), col))
    row = lambda bi, i: (0, 0)
    tok = lambda w: pl.BlockSpec((1, ts, w), lambda bi, i: (bi, i, 0))
    args = [z3, z3, z3, hist, conv_w, qg, kvg, cs_tab]
    in_specs = [tok(n), halo(1), halo(2),
                pl.BlockSpec((1, HALO_C, dc), lambda bi, i: (bi, 0, 0)),
                pl.BlockSpec((CONV_C, dc), row),
                pl.BlockSpec((1, rq), row), pl.BlockSpec((1, rkv), row),
                pl.BlockSpec((ts, LANES), lambda bi, i: (i, 0))]
    if wq_t is None:
        q_spec, q_shape = tok(rq), jax.ShapeDtypeStruct((b, s, rq), BF16)
    else:
        args += [wq_t, cs_tab.T]
        in_specs += [pl.BlockSpec(wq_t.shape, row), pl.BlockSpec((LANES, ts), lambda bi, i: (0, i))]
        q_spec = pl.BlockSpec((1, N_HEADS, 1, HEAD_PAD, ts), lambda bi, i: (bi, 0, i, 0, 0))
        q_shape = jax.ShapeDtypeStruct((b, N_HEADS, s // ts, HEAD_PAD, ts), BF16)
    return pl.pallas_call(
        functools.partial(_odd_mid_body, ts=ts, dc=dc, rq=rq, rkv=rkv, q_scale=q_scale),
        grid=(b, s // ts),
        in_specs=in_specs,
        out_specs=[tok(dc), q_spec, tok(rkv), tok(LANES),
                   pl.BlockSpec((1, HALO_C, dc), lambda bi, i: (bi, 0, 0))],
        out_shape=[jax.ShapeDtypeStruct((b, s, dc), BF16),
                   q_shape,
                   jax.ShapeDtypeStruct((b, s, rkv), F32),
                   jax.ShapeDtypeStruct((b, s, LANES), F32),
                   jax.ShapeDtypeStruct((b, HALO_C, dc), F32)],
        scratch_shapes=[pltpu.VMEM((HALO_C + ts, dc), F32)],
        compiler_params=_params("parallel", "arbitrary"),
        name="odd_mid",
    )(*args)


def _kv_up_body(c_ref, kr_ref, wk_ref, wvt_ref, k_ref, vt_ref):
    c = c_ref[...].astype(BF16)
    k = jnp.dot(c, wk_ref[...], preferred_element_type=F32)
    vt = lax.dot_general(wvt_ref[...], c, (((1,), (1,)), ((), ())), preferred_element_type=F32)
    kr = kr_ref[...].astype(BF16)
    tk = vt_ref.shape[-1]
    for h in range(k.shape[1] // NOPE_DIM):
        k_ref[:, h * HEAD_PAD:h * HEAD_PAD + NOPE_DIM] = k[:, h * NOPE_DIM:(h + 1) * NOPE_DIM].astype(BF16)
        k_ref[:, h * HEAD_PAD + NOPE_DIM:(h + 1) * HEAD_PAD] = kr
        for t in range(vt_ref.shape[2]):
            vt_ref[0, h, t, 0:V_DIM, :] = vt[h * V_DIM:(h + 1) * V_DIM, t * tk:(t + 1) * tk].astype(BF16)
            vt_ref[0, h, t, V_DIM:V_ROWS, :] = jnp.ones((V_ROWS - V_DIM, tk), BF16)


def _kv_up(c, kr, wk, wv_t, b, skv, tk):
    t, r = c.shape
    nkt = skv // tk
    tm = _tile(skv, KV_ROWS_STEP)
    kts = tm // tk
    per = skv // tm
    hpt = 4
    return pl.pallas_call(
        _kv_up_body,
        grid=(t // tm, N_HEADS // hpt),
        in_specs=[pl.BlockSpec((tm, r), lambda i, j: (i, 0)),
                  pl.BlockSpec((tm, LANES), lambda i, j: (i, 0)),
                  pl.BlockSpec((r, hpt * NOPE_DIM), lambda i, j: (0, j)),
                  pl.BlockSpec((hpt * V_DIM, r), lambda i, j: (j, 0))],
        out_specs=[pl.BlockSpec((tm, hpt * HEAD_PAD), lambda i, j: (i, j)),
                   pl.BlockSpec((1, hpt, kts, V_ROWS, tk), lambda i, j: (i // per, j, i % per, 0, 0))],
        out_shape=[jax.ShapeDtypeStruct((t, N_HEADS * HEAD_PAD), BF16),
                   jax.ShapeDtypeStruct((b, N_HEADS, nkt, V_ROWS, tk), BF16)],
        compiler_params=_params("parallel", "arbitrary"),
        name="kv_up",
    )(c, kr, wk, wv_t)


def _colmax(s):
    while s.shape[0] > 8 and s.shape[0] % 16 == 0:
        half = s.shape[0] // 2
        s = jnp.maximum(s[:half], s[half:])
    return jnp.max(s, axis=0, keepdims=True)


def _attention_body(qt_ref, k_ref, vt_ref, o_ref, s_ref, p_ref, acc_ref, *, tq, tk, past, t_valid):
    def k_tile(kt):
        return k_ref[0, kt * tk:(kt + 1) * tk, :]

    def half_step(kt, n_steps, carry, qt, q0, masked):
        m, alpha = carry
        slot, other = kt % 2, 1 - kt % 2
        if kt >= 1:
            pv = jnp.dot(vt_ref[0, 0, kt - 1], p_ref[other], preferred_element_type=F32)
        if kt + 1 < n_steps:
            s_next = jnp.dot(k_tile(kt + 1), qt, preferred_element_type=F32)
        s = s_ref[slot]
        if masked:
            k_pos = kt * tk + lax.broadcasted_iota(jnp.int32, (tk, 1), 0)
            q_pos = q0 + lax.broadcasted_iota(jnp.int32, (1, tq), 1)
            k_chunk = jnp.where(k_pos < t_valid, k_pos >> CHUNK_SHIFT, jnp.iinfo(jnp.int32).max)
            s = jnp.where(k_chunk <= q_pos >> CHUNK_SHIFT, s, MASK_VALUE)
        m_new = jnp.maximum(m, _colmax(s))
        p_ref[slot] = jnp.exp2(s - m_new).astype(BF16)
        if kt == 1:
            acc_ref[...] = pv
        elif kt > 1:
            acc_ref[...] = alpha * acc_ref[...] + pv
        if kt + 1 < n_steps:
            s_ref[other] = s_next
        return m_new, jnp.exp2(m - m_new)

    for qi in range(qt_ref.shape[2]):
        qt = qt_ref[0, 0, qi]
        q0 = past + qi * tq
        n_full = min(((q0 >> CHUNK_SHIFT) + 1) * CHUNK, t_valid) // tk
        n_all = -(-min((((q0 + tq - 1) >> CHUNK_SHIFT) + 1) * CHUNK, t_valid) // tk)
        s_ref[0] = jnp.dot(k_tile(0), qt, preferred_element_type=F32)
        carry = (jnp.full((1, tq), MASK_VALUE, F32), None)
        for kt in range(n_all):
            carry = half_step(kt, n_all, carry, qt=qt, q0=q0, masked=kt >= n_full)
        acc = jnp.dot(vt_ref[0, 0, n_all - 1], p_ref[(n_all - 1) % 2], preferred_element_type=F32)
        if n_all >= 2:
            acc = carry[1] * acc_ref[...] + acc
        o_ref[0, qi * tq:(qi + 1) * tq, :] = jnp.transpose(acc[:V_DIM] / acc[V_DIM:V_DIM + 1]).astype(BF16)


def _attention(qt, k3, vt, past, t_valid):
    b, _, nq, _, tq = qt.shape
    skv = k3.shape[1]
    nkt, tk = vt.shape[2], vt.shape[4]
    return pl.pallas_call(
        functools.partial(_attention_body, tq=tq, tk=tk, past=past, t_valid=t_valid),
        scratch_shapes=[pltpu.VMEM((2, tk, tq), F32), pltpu.VMEM((2, tk, tq), BF16), pltpu.VMEM((V_ROWS, tq), F32)],
        grid=(b, N_HEADS),
        in_specs=[pl.BlockSpec((1, 1, nq, HEAD_PAD, tq), lambda bi, h: (bi, h, 0, 0, 0)),
                  pl.BlockSpec((1, skv, HEAD_PAD), lambda bi, h: (bi, 0, h)),
                  pl.BlockSpec((1, 1, nkt, V_ROWS, tk), lambda bi, h: (bi, h, 0, 0, 0))],
        out_specs=pl.BlockSpec((1, nq * tq, V_DIM), lambda bi, h: (bi, 0, h)),
        out_shape=jax.ShapeDtypeStruct((b, nq * tq, N_HEADS * V_DIM), BF16),
        compiler_params=_params("parallel", "parallel"),
        name="attention",
    )(qt, k3, vt)


def _latent_attention_body(qn_ref, wq_ref, wuk_ref, cs_ref, lat_ref, krp_ref, cnew_ref, krnew_ref, wuv_ref,
                           o_ref, q_ref, acc_ref, *, ss, tk, n_cache, past, scale, rkv):
    qn = qn_ref[0]
    cs = cs_ref[...]
    for h in range(N_HEADS):
        qh = jnp.dot(qn, wq_ref[:, h * HEAD_PAD:(h + 1) * HEAD_PAD], preferred_element_type=F32)
        q_abs = lax.dot_general(qh[:, :NOPE_DIM].astype(BF16), wuk_ref[:, h * NOPE_DIM:(h + 1) * NOPE_DIM],
                                (((1,), (1,)), ((), ())), preferred_element_type=F32)
        q_ref[h * ss:(h + 1) * ss, 0:rkv] = (q_abs * scale).astype(BF16)
        q_ref[h * ss:(h + 1) * ss, rkv:rkv + LANES] = (_rope_fold(qh[:, NOPE_DIM:], cs) * scale).astype(BF16)

    q_lat = q_ref[:, 0:rkv]
    q_rope = q_ref[:, rkv:rkv + ROPE_DIM]
    cols = N_HEADS * ss
    q_chunk = (past + lax.broadcasted_iota(jnp.int32, (1, cols), 1) % ss) >> CHUNK_SHIFT
    acc_ref[...] = jnp.zeros((rkv, cols), F32)
    nt = (((1,), (1,)), ((), ()))

    def update(c, kr, k0, carry):
        m, l = carry
        c = c.astype(BF16)
        s = (lax.dot_general(c, q_lat, nt, preferred_element_type=F32)
             + lax.dot_general(kr.astype(BF16), q_rope, nt, preferred_element_type=F32))
        k_chunk = (k0 + lax.broadcasted_iota(jnp.int32, (c.shape[0], 1), 0)) >> CHUNK_SHIFT
        s = jnp.where(k_chunk <= q_chunk, s, MASK_VALUE)
        m_new = jnp.maximum(m, _colmax(s))
        alpha = jnp.exp2(m - m_new)
        p = jnp.exp2(s - m_new)
        pv = lax.dot_general(c, p.astype(BF16), (((0,), (0,)), ((), ())), preferred_element_type=F32)
        acc_ref[...] = alpha * acc_ref[...] + pv
        return m_new, alpha * l + jnp.sum(p, axis=0, keepdims=True)

    def cache_step(kt, carry):
        rows = pl.ds(pl.multiple_of(kt * tk, tk), tk)
        return update(lat_ref[0, rows, :], krp_ref[0, rows, :], kt * tk, carry)

    carry = (jnp.full((1, cols), MASK_VALUE, F32), jnp.zeros((1, cols), F32))
    carry = lax.fori_loop(0, n_cache, cache_step, carry)
    _, l = update(cnew_ref[0], krnew_ref[0, :, 0:ROPE_DIM], past, carry)
    o_lat = jnp.transpose(acc_ref[...] / l).astype(BF16)
    for h in range(N_HEADS):
        o_ref[0, :, h * V_DIM:(h + 1) * V_DIM] = jnp.dot(
            o_lat[h * ss:(h + 1) * ss], wuv_ref[:, h * V_DIM:(h + 1) * V_DIM],
            preferred_element_type=F32).astype(BF16)


def _latent_attention(qn, wq, wuk, cs_tab, lat_prev, kr_prev, c_new, kr_new, wuv, scale):
    b, ss, rq = qn.shape
    past, rkv = lat_prev.shape[1], lat_prev.shape[2]
    tk = min(TK_ATT, past)
    see_cache = min((((past + ss - 1) >> CHUNK_SHIFT) + 1) * CHUNK, past)
    n_cache = -(-see_cache // tk)
    assert past % tk == 0
    whole = lambda a: pl.BlockSpec(a.shape, lambda bi: (0,) * a.ndim)
    per_b = lambda a: pl.BlockSpec((1,) + a.shape[1:], lambda bi: (bi, 0, 0))
    return pl.pallas_call(
        functools.partial(_latent_attention_body, ss=ss, tk=tk, n_cache=n_cache, past=past, scale=scale, rkv=rkv),
        grid=(b,),
        in_specs=[per_b(qn), whole(wq), whole(wuk), whole(cs_tab),
                  per_b(lat_prev), per_b(kr_prev), per_b(c_new), per_b(kr_new), whole(wuv)],
        out_specs=pl.BlockSpec((1, ss, N_HEADS * V_DIM), lambda bi: (bi, 0, 0)),
        out_shape=jax.ShapeDtypeStruct((b, ss, N_HEADS * V_DIM), BF16),
        scratch_shapes=[pltpu.VMEM((N_HEADS * ss, rkv + LANES), BF16), pltpu.VMEM((rkv, N_HEADS * ss), F32)],
        compiler_params=_params("parallel"),
        name="latent_attention",
    )(qn, wq, wuk, cs_tab, lat_prev, kr_prev, c_new, kr_new, wuv)


def _rope_table(past, s):
    half = ROPE_DIM // 2
    inv = ROPE_THETA ** (-jnp.arange(half, dtype=F32) / half)
    ang = (past + jnp.arange(s, dtype=jnp.int32)).astype(F32)[:, None] * inv[None, :]
    cos, sin = jnp.cos(ang), jnp.sin(ang)
    return jnp.concatenate([cos, cos, sin, sin], axis=-1)


def _rot_cols(w):
    half = ROPE_DIM // 2
    return jnp.concatenate([-w[..., half:], w[..., :half]], axis=-1)


def _pad_rows_front(x, rows):
    return jnp.pad(x, ((0, 0), (rows - x.shape[1], 0), (0, 0)))


def _even_layer(h, b, s, hist, wts, want_v):
    (g_mix, w_in, conv_w, conv_b, lag, lab, lvg, lvb, w_sp, b_sp_t, w_out) = wts
    z = _norm_matmul(h, g_mix, w_in, _tile(w_in.shape[1], TN_IN_EVEN))
    outs = _even_mid(z.reshape(b, s, -1), hist, conv_w, conv_b, lag, lab, lvg, lvb, w_sp, b_sp_t, want_v)
    cat, tail = outs[0], outs[1]
    h = _proj_residual([cat.reshape(b * s, -1)], [w_out], h)
    return h, tail[:, HALO_A - (CONV_A - 1):], (outs[2] if want_v else None)


def _odd_layer(h, b, s, hist, lat_prev, kr_prev, wts, cs_tab):
    (g_mix, w_in, conv_w, qg, wq, kvg, w_k, w_v, w_out_c, w_out_a) = wts
    dc = conv_w.shape[1]
    rq, rkv = qg.shape[1], kvg.shape[1]
    z = _norm_matmul(h, g_mix, w_in, w_in.shape[1] // 3)
    scale = math.log2(math.e) / math.sqrt(NOPE_DIM + ROPE_DIM)
    if lat_prev is None:
        assert s % LANES == 0 and TS_ODD == TQ_ATT
        cout, qt, ckv, kr, tail = _odd_mid(z.reshape(b, s, -1), hist, conv_w, qg, kvg, cs_tab, dc, rq, rkv,
                                           wq.T, scale)
        tk = min(TK_ATT, s)
        k, vt = _kv_up(ckv.reshape(b * s, rkv), kr.reshape(b * s, LANES), w_k, w_v.T, b, s, tk)
        attn = _attention(qt, k.reshape(b, s, -1), vt, 0, s)
    else:
        cout, qn, ckv, kr, tail = _odd_mid(z.reshape(b, s, -1), hist, conv_w, qg, kvg, cs_tab, dc, rq, rkv)
        attn = _latent_attention(qn, wq, w_k, cs_tab, lat_prev, kr_prev, ckv, kr, w_v, scale)
    h = _proj_residual([cout.reshape(b * s, dc), attn.reshape(b * s, -1)], [w_out_c, w_out_a], h)
    return h, tail[:, HALO_C - (CONV_C - 1):], ckv, kr[..., :ROPE_DIM]


def kernel(x_prompt, x_sample, state_conv_a, state_conv_c, cache_kv_latent, cache_k_rope, norm_mix, norm_ffn, norm_final, w_in_even, conv_a_w, conv_a_b, ln_a_g, ln_a_b, ln_v_g, ln_v_b, w_spatial, b_spatial, w_out_even, w_in_odd, conv_c_w, q_norm_g, w_uq, kv_norm_g, w_ukv, w_out_odd, w_ffn_up, w_ffn_down):
    bp, sp, d = x_prompt.shape
    bs, ss, _ = x_sample.shape
    depth = norm_mix.shape[0]
    da = conv_a_w.shape[-1]
    dc = conv_c_w.shape[-1]
    rq, rkv = q_norm_g.shape[-1], kv_norm_g.shape[-1]
    past = cache_kv_latent.shape[2]
    assert sp % GMLP_CHUNK == 0 and ss <= GMLP_CHUNK and ss >= HALO_A and past % GMLP_CHUNK == 0

    hp = x_prompt.reshape(bp * sp, d)
    hs = x_sample.reshape(bs * ss, d)
    cs_p = _rope_table(0, sp)
    cs_s = _rope_table(past, ss)
    wu_all, wd_all = _cast_bf16(w_ffn_up), _cast_bf16(w_ffn_down)
    w_in_even_bf, w_out_even_bf, w_out_odd_bf = _cast_bf16(w_in_even), _cast_bf16(w_out_even), _cast_bf16(w_out_odd)

    ca_p, ca_s, gv_s, cc_p, cc_s, lat_p, kr_p, lat_s, kr_s = [], [], [], [], [], [], [], [], []
    for l in range(depth):
        p = l // 2
        if l % 2 == 0:
            row = lambda a: a[p].reshape(1, -1)
            gl_p, gl_s = min(sp, GMLP_CHUNK), min(ss, GMLP_CHUNK)
            common = (norm_mix[l], w_in_even_bf[p],
                      jnp.pad(conv_a_w[p], ((0, CONV_A_ROWS - CONV_A), (0, 0))), row(conv_a_b),
                      row(ln_a_g), row(ln_a_b), row(ln_v_g), row(ln_v_b))
            w_out = w_out_even_bf[p]
            wts_p = common + (w_spatial[p][:, :gl_p, :gl_p], b_spatial[p][:, :gl_p].T, w_out)
            wts_s = common + (w_spatial[p][:, :gl_s, :gl_s], b_spatial[p][:, :gl_s].T, w_out)
            hp, tail_p, _ = _even_layer(hp, bp, sp, jnp.zeros((bp, HALO_A, da), F32), wts_p, False)
            hs, tail_s, v_s = _even_layer(hs, bs, ss, _pad_rows_front(state_conv_a[p], HALO_A), wts_s, True)
            ca_p.append(tail_p)
            ca_s.append(tail_s)
            gv_s.append(v_s)
        else:
            w_in = w_in_odd[p]
            n_main = 3 * dc + rq + rkv
            w_in = jnp.concatenate([w_in, _rot_cols(w_in[:, n_main:])], axis=1).astype(BF16)
            wq = w_uq[p].reshape(rq, N_HEADS, NOPE_DIM + ROPE_DIM)
            wq = jnp.concatenate([wq, _rot_cols(wq[..., NOPE_DIM:])], axis=-1).reshape(rq, N_HEADS * HEAD_PAD)
            wq = wq.astype(BF16)
            wkv = w_ukv[p].reshape(rkv, N_HEADS, NOPE_DIM + V_DIM)
            wk = wkv[..., :NOPE_DIM].reshape(rkv, N_HEADS * NOPE_DIM).astype(BF16)
            wv = wkv[..., NOPE_DIM:].reshape(rkv, N_HEADS * V_DIM).astype(BF16)
            w_out = w_out_odd_bf[p]
            wts = (norm_mix[l], w_in, conv_c_w[p], q_norm_g[p].reshape(1, rq), wq,
                   kv_norm_g[p].reshape(1, rkv), wk, wv, w_out[:dc], w_out[dc:])
            hp, tail_p, c_p, r_p = _odd_layer(hp, bp, sp, jnp.zeros((bp, HALO_C, dc), F32), None, None,
                                              wts, cs_p)
            hs, tail_s, c_s, r_s = _odd_layer(hs, bs, ss, _pad_rows_front(state_conv_c[p], HALO_C),
                                              cache_kv_latent[p], cache_k_rope[p], wts, cs_s)
            cc_p.append(tail_p)
            cc_s.append(tail_s)
            lat_p.append(c_p)
            kr_p.append(r_p)
            lat_s.append(c_s)
            kr_s.append(r_s)
        last = l == depth - 1
        hp = _ffn(hp, norm_ffn[l], wu_all, wd_all, l, norm_final, last)
        hs = _ffn(hs, norm_ffn[l], wu_all, wd_all, l, norm_final, last)

    return (hp.reshape(bp, sp, d), hs.reshape(bs, ss, d),
            jnp.stack(ca_p), jnp.stack(ca_s), jnp.stack(gv_s), jnp.stack(cc_p), jnp.stack(cc_s),
            jnp.stack(lat_p), jnp.stack(kr_p), jnp.stack(lat_s), jnp.stack(kr_s))
```

```python
import functools
import math

import jax
import jax.numpy as jnp
from jax import lax
from jax.experimental import pallas as pl
from jax.experimental.pallas import tpu as pltpu

F32 = jnp.float32
BF16 = jnp.bfloat16

CHUNK = 64
CHUNK_SHIFT = 6
CONV_A = 31
CONV_C = 3
G_B = 8
GMLP_CHUNK = 128
N_HEADS = 16
NOPE_DIM = 128
ROPE_DIM = 64
V_DIM = 128
V_ROWS = V_DIM + 16
ROPE_THETA = 10000.0
EPS = 1e-6

LANES = 128
SUBLANES = 8
HEAD_PAD = 2 * LANES
HALO_A = 32
HALO_C = 8
CONV_A_ROWS = -(-CONV_A // SUBLANES) * SUBLANES
MASK_VALUE = -1e30
VMEM_LIMIT = 60 * 1024 * 1024

TM_MATMUL = 1024
TN_MATMUL = 1024
TF_FFN = 1024
TS_EVEN = 256
TS_ODD = 512
TQ_ATT = 512
TK_ATT = 512
CAST_BLOCK = (1024, 2048)
KV_ROWS_STEP = 1024
TN_EVEN = 2048


def _params(*sem):
    return pltpu.CompilerParams(dimension_semantics=sem, vmem_limit_bytes=VMEM_LIMIT)


def _rms(x, g):
    return x * lax.rsqrt(jnp.mean(x * x, axis=-1, keepdims=True) + EPS) * g


def _layer_norm(x, g, b):
    xc = x - jnp.mean(x, axis=-1, keepdims=True)
    return xc * lax.rsqrt(jnp.mean(xc * xc, axis=-1, keepdims=True) + EPS) * g + b


def _sigmoid(x):
    return 0.5 * jnp.tanh(0.5 * x) + 0.5


def _gelu(x):
    return 0.5 * x * (1.0 + lax.erf(x * math.sqrt(0.5)))


def _tile(n, t):
    t = min(n, t)
    assert n % t == 0, (n, t)
    return t


def _norm_matmul_body(x_ref, g_ref, w_ref, o_ref, xn_ref):
    @pl.when(pl.program_id(1) == 0)
    def _():
        xn_ref[...] = _rms(x_ref[...], g_ref[...]).astype(BF16)

    o_ref[...] = jnp.dot(xn_ref[...], w_ref[...], preferred_element_type=F32)


def _norm_matmul(x, g, w, tn):
    t, d = x.shape
    n = w.shape[1]
    tm = _tile(t, TM_MATMUL)
    return pl.pallas_call(
        _norm_matmul_body,
        grid=(t // tm, n // tn),
        in_specs=[pl.BlockSpec((tm, d), lambda i, j: (i, 0)),
                  pl.BlockSpec((1, d), lambda i, j: (0, 0)),
                  pl.BlockSpec((d, tn), lambda i, j: (0, j))],
        out_specs=pl.BlockSpec((tm, tn), lambda i, j: (i, j)),
        out_shape=jax.ShapeDtypeStruct((t, n), F32),
        scratch_shapes=[pltpu.VMEM((tm, d), BF16)],
        compiler_params=_params("parallel", "arbitrary"),
        name="norm_matmul",
    )(x, g.reshape(1, d), w)


def _even_mid_body(*refs, ts, da, gl, want_v):
    (zc_ref, zh_ref, hist_ref, cw_ref, cb_ref, lag_ref, lab_ref, lvg_ref, lvb_ref,
     wsp_ref, bsp_ref, cat_ref, tail_ref) = refs[:13]
    v_ref = refs[13] if want_v else None
    buf_ref, conv_ref = refs[-2:]
    i = pl.program_id(1)

    buf_ref[HALO_A:HALO_A + ts, :] = zc_ref[0, :, 0:da] * _sigmoid(zc_ref[0, :, da:2 * da])

    @pl.when(i == 0)
    def _():
        buf_ref[0:HALO_A, :] = hist_ref[0]

    @pl.when(i > 0)
    def _():
        buf_ref[0:HALO_A, :] = zh_ref[0, :, 0:da] * _sigmoid(zh_ref[0, :, da:2 * da])

    tail_ref[0] = buf_ref[ts:ts + HALO_A, :]

    rc = min(ts, 128)
    base = HALO_A - (CONV_A - 1)
    win = rc + HALO_A
    for c in range(da // LANES):
        cs = slice(c * LANES, (c + 1) * LANES)
        for r0 in range(0, ts, rc):
            window = buf_ref[r0:r0 + win, cs]
            acc = jnp.broadcast_to(cb_ref[:, cs], (rc, LANES))
            for sh in range(SUBLANES):
                rolled = window if sh == 0 else pltpu.roll(window, win - sh, axis=0)
                for off in range(sh, HALO_A + 1, SUBLANES):
                    j = off - base
                    if 0 <= j < CONV_A:
                        acc = acc + cw_ref[j:j + 1, cs] * rolled[off - sh:off - sh + rc]
            conv_ref[r0:r0 + rc, cs] = acc

    y = _layer_norm(conv_ref[...], lag_ref[...], lab_ref[...])
    cat_ref[0, :, 0:da] = (y * _sigmoid(y)).astype(BF16)

    u = _gelu(zc_ref[0, :, 2 * da:3 * da])
    v = _layer_norm(_gelu(zc_ref[0, :, 3 * da:4 * da]), lvg_ref[...], lvb_ref[...])
    if want_v:
        v_ref[0] = v
    vb = v.astype(BF16)
    dh = da // G_B
    tril = lax.broadcasted_iota(jnp.int32, (gl, gl), 0) >= lax.broadcasted_iota(jnp.int32, (gl, gl), 1)
    for g in range(G_B):
        wg = jnp.where(tril, wsp_ref[g], 0.0).astype(BF16)
        gs = slice(g * dh, (g + 1) * dh)
        for k in range(ts // gl):
            rs = slice(k * gl, (k + 1) * gl)
            mixed = jnp.dot(wg, vb[rs, gs], preferred_element_type=F32) + bsp_ref[:, g:g + 1]
            cat_ref[0, rs, da + g * dh:da + (g + 1) * dh] = (u[rs, gs] * mixed).astype(BF16)


def _even_mid(z3, hist, conv_w, conv_b, lag, lab, lvg, lvb, w_sp, b_sp_t, want_v):
    b, s, n4 = z3.shape
    da = n4 // 4
    gl = w_sp.shape[-1]
    ts = _tile(s, TS_EVEN)
    assert ts % gl == 0 and ts % HALO_A == 0
    hb = ts // HALO_A
    row = lambda bi, i: (0, 0)
    out_shape = [jax.ShapeDtypeStruct((b, s, 2 * da), BF16), jax.ShapeDtypeStruct((b, HALO_A, da), F32)]
    out_specs = [pl.BlockSpec((1, ts, 2 * da), lambda bi, i: (bi, i, 0)),
                 pl.BlockSpec((1, HALO_A, da), lambda bi, i: (bi, 0, 0))]
    if want_v:
        out_shape.append(jax.ShapeDtypeStruct((b, s, da), F32))
        out_specs.append(pl.BlockSpec((1, ts, da), lambda bi, i: (bi, i, 0)))
    return pl.pallas_call(
        functools.partial(_even_mid_body, ts=ts, da=da, gl=gl, want_v=want_v),
        grid=(b, s // ts),
        in_specs=[pl.BlockSpec((1, ts, n4), lambda bi, i: (bi, i, 0)),
                  pl.BlockSpec((1, HALO_A, 2 * da), lambda bi, i: (bi, jnp.maximum(i * hb - 1, 0), 0)),
                  pl.BlockSpec((1, HALO_A, da), lambda bi, i: (bi, 0, 0)),
                  pl.BlockSpec((CONV_A_ROWS, da), row),
                  pl.BlockSpec((1, da), row), pl.BlockSpec((1, da), row), pl.BlockSpec((1, da), row),
                  pl.BlockSpec((1, da), row), pl.BlockSpec((1, da), row),
                  pl.BlockSpec((G_B, gl, gl), lambda bi, i: (0, 0, 0)),
                  pl.BlockSpec((gl, G_B), row)],
        out_specs=out_specs,
        out_shape=out_shape,
        scratch_shapes=[pltpu.VMEM((HALO_A + ts, da), F32), pltpu.VMEM((ts, da), F32)],
        compiler_params=_params("parallel", "arbitrary"),
        name="even_mid",
    )(z3, z3, hist, conv_w, conv_b, lag, lab, lvg, lvb, w_sp, b_sp_t)


def _proj_residual_body(*refs, n_in):
    x_refs, w_refs = refs[:n_in], refs[n_in:2 * n_in]
    r_ref, o_ref = refs[2 * n_in], refs[2 * n_in + 1]
    acc = r_ref[...]
    for x_ref, w_ref in zip(x_refs, w_refs):
        acc = acc + jnp.dot(x_ref[...], w_ref[...], preferred_element_type=F32)
    o_ref[...] = acc


def _proj_residual(xs, ws, res, tn_max=TN_MATMUL):
    t, n = res.shape
    tm = _tile(t, TM_MATMUL)
    tn = _tile(n, tn_max)
    in_specs = [pl.BlockSpec((tm, x.shape[1]), lambda i, j: (i, 0)) for x in xs]
    in_specs += [pl.BlockSpec((w.shape[0], tn), lambda i, j: (0, j)) for w in ws]
    in_specs.append(pl.BlockSpec((tm, tn), lambda i, j: (i, j)))
    return pl.pallas_call(
        functools.partial(_proj_residual_body, n_in=len(xs)),
        grid=(t // tm, n // tn),
        in_specs=in_specs,
        out_specs=pl.BlockSpec((tm, tn), lambda i, j: (i, j)),
        out_shape=jax.ShapeDtypeStruct((t, n), F32),
        compiler_params=_params("parallel", "arbitrary"),
        name="proj_residual",
    )(*xs, *ws, res)


def _ffn_body(x_ref, g_ref, wu_ref, wd_ref, gf_ref, o_ref, xn_ref, *, final_norm):
    j = pl.program_id(1)

    @pl.when(j == 0)
    def _():
        x = x_ref[...]
        xn_ref[...] = _rms(x, g_ref[...]).astype(BF16)
        o_ref[...] = x

    a = jnp.maximum(jnp.dot(xn_ref[...], wu_ref[...], preferred_element_type=F32), 0.0)
    o_ref[...] += jnp.dot((a * a).astype(BF16), wd_ref[...], preferred_element_type=F32)

    if final_norm:
        @pl.when(j == pl.num_programs(1) - 1)
        def _():
            o_ref[...] = _rms(o_ref[...], gf_ref[...])


def _cast_bf16_body(w_ref, o_ref):
    o_ref[...] = w_ref[...].astype(BF16)


def _cast_bf16(w):
    nl, r, c = w.shape
    tr, tc = _tile(r, CAST_BLOCK[0]), _tile(c, CAST_BLOCK[1])
    spec = pl.BlockSpec((1, tr, tc), lambda l, i, j: (l, i, j))
    return pl.pallas_call(
        _cast_bf16_body,
        grid=(nl, r // tr, c // tc),
        in_specs=[spec],
        out_specs=spec,
        out_shape=jax.ShapeDtypeStruct(w.shape, BF16),
        compiler_params=_params("parallel", "parallel", "parallel"),
        name="cast_bf16",
    )(w)


def _ffn(x, g, wu, wd, layer, gf, final_norm):
    t, d = x.shape
    f = wu.shape[2]
    tm = _tile(t, TM_MATMUL)
    tf = _tile(f, TF_FFN)
    return pl.pallas_call(
        functools.partial(_ffn_body, final_norm=final_norm),
        grid=(t // tm, f // tf),
        in_specs=[pl.BlockSpec((tm, d), lambda i, j: (i, 0)),
                  pl.BlockSpec((1, d), lambda i, j: (0, 0)),
                  pl.BlockSpec((None, d, tf), lambda i, j: (layer, 0, j)),
                  pl.BlockSpec((None, tf, d), lambda i, j: (layer, j, 0)),
                  pl.BlockSpec((1, d), lambda i, j: (0, 0))],
        out_specs=pl.BlockSpec((tm, d), lambda i, j: (i, 0)),
        out_shape=jax.ShapeDtypeStruct((t, d), F32),
        scratch_shapes=[pltpu.VMEM((tm, d), BF16)],
        compiler_params=_params("parallel", "arbitrary"),
        name="ffn",
    )(x, g.reshape(1, d), wu, wd, gf.reshape(1, d))


def _rope_fold(y, cs):
    t = y * cs
    return t + pltpu.roll(t, ROPE_DIM, axis=1)


def _q_heads_store(wq_ref, qn, cs_t, qt_ref, scale):
    hpg = 4
    for g in range(N_HEADS // hpg):
        acc = lax.dot_general(wq_ref[g * hpg * HEAD_PAD:(g + 1) * hpg * HEAD_PAD, :], qn,
                              (((1,), (1,)), ((), ())), preferred_element_type=F32)
        for hh in range(hpg):
            h, lo = g * hpg + hh, hh * HEAD_PAD
            qt_ref[0, h, 0, 0:NOPE_DIM, :] = (acc[lo:lo + NOPE_DIM] * scale).astype(BF16)
            t = acc[lo + NOPE_DIM:lo + HEAD_PAD] * cs_t
            qt_ref[0, h, 0, NOPE_DIM:NOPE_DIM + ROPE_DIM, :] = ((t[:ROPE_DIM] + t[ROPE_DIM:]) * scale).astype(BF16)
            qt_ref[0, h, 0, NOPE_DIM + ROPE_DIM:HEAD_PAD, :] = jnp.zeros((ROPE_DIM, acc.shape[1]), BF16)


def _odd_mid_body(*refs, ts, dc, rq, rkv, q_scale):
    zc_ref, hc_ref, hx_ref, hist_ref, cw_ref, qg_ref, kvg_ref, cs_ref = refs[:8]
    if q_scale is not None:
        wq_ref, cst_ref = refs[8:10]
    cout_ref, q_out_ref, ckv_ref, kr_ref, tail_ref, buf_ref = refs[-6:]
    i = pl.program_id(1)
    gx = zc_ref[0, :, dc:2 * dc] * zc_ref[0, :, 2 * dc:3 * dc]
    buf_ref[HALO_C:HALO_C + ts, :] = gx

    @pl.when(i == 0)
    def _():
        buf_ref[0:HALO_C, :] = hist_ref[0]

    @pl.when(i > 0)
    def _():
        buf_ref[0:HALO_C, :] = hc_ref[0] * hx_ref[0]

    tail_ref[0] = buf_ref[ts:ts + HALO_C, :]
    conv = cw_ref[CONV_C - 1:CONV_C, :] * gx
    for j in range(CONV_C - 1):
        off = HALO_C - (CONV_C - 1) + j
        conv = conv + cw_ref[j:j + 1, :] * buf_ref[off:off + ts, :]
    cout_ref[0] = (zc_ref[0, :, 0:dc] * conv).astype(BF16)

    o = 3 * dc
    qn = _rms(zc_ref[0, :, o:o + rq], qg_ref[...]).astype(BF16)
    if q_scale is None:
        q_out_ref[0] = qn
    else:
        _q_heads_store(wq_ref, qn, cst_ref[...], q_out_ref, q_scale)
    ckv_ref[0] = _rms(zc_ref[0, :, o + rq:o + rq + rkv], kvg_ref[...])
    kr = _rope_fold(zc_ref[0, :, o + rq + rkv:o + rq + rkv + LANES], cs_ref[...])
    lane = lax.broadcasted_iota(jnp.int32, kr.shape, 1)
    kr_ref[0] = jnp.where(lane < ROPE_DIM, kr, 0.0)


def _odd_mid(z3, hist, conv_w, qg, kvg, cs_tab, dc, rq, rkv, wq_t=None, q_scale=None):
    b, s, n = z3.shape
    ts = _tile(s, TS_ODD)
    assert ts % HALO_C == 0 and n == 3 * dc + rq + rkv + LANES and dc % 1024 == 0
    hb = ts // HALO_C
    halo = lambda col: pl.BlockSpec((1, HALO_C, dc), lambda bi, i: (bi, jnp.maximum(i * hb - 1, 0), col))
    row = lambda bi, i: (0, 0)
    tok = lambda w: pl.BlockSpec((1, ts, w), lambda bi, i: (bi, i, 0))
    args = [z3, z3, z3, hist, conv_w, qg, kvg, cs_tab]
    in_specs = [tok(n), halo(1), halo(2),
                pl.BlockSpec((1, HALO_C, dc), lambda bi, i: (bi, 0, 0)),
                pl.BlockSpec((CONV_C, dc), row),
                pl.BlockSpec((1, rq), row), pl.BlockSpec((1, rkv), row),
                pl.BlockSpec((ts, LANES), lambda bi, i: (i, 0))]
    if wq_t is None:
        q_spec, q_shape = tok(rq), jax.ShapeDtypeStruct((b, s, rq), BF16)
    else:
        args += [wq_t, cs_tab.T]
        in_specs += [pl.BlockSpec(wq_t.shape, row), pl.BlockSpec((LANES, ts), lambda bi, i: (0, i))]
        q_spec = pl.BlockSpec((1, N_HEADS, 1, HEAD_PAD, ts), lambda bi, i: (bi, 0, i, 0, 0))
        q_shape = jax.ShapeDtypeStruct((b, N_HEADS, s // ts, HEAD_PAD, ts), BF16)
    return pl.pallas_call(
        functools.partial(_odd_mid_body, ts=ts, dc=dc, rq=rq, rkv=rkv, q_scale=q_scale),
        grid=(b, s // ts),
        in_specs=in_specs,
        out_specs=[tok(dc), q_spec, tok(rkv), tok(LANES),
                   pl.BlockSpec((1, HALO_C, dc), lambda bi, i: (bi, 0, 0))],
        out_shape=[jax.ShapeDtypeStruct((b, s, dc), BF16),
                   q_shape,
                   jax.ShapeDtypeStruct((b, s, rkv), F32),
                   jax.ShapeDtypeStruct((b, s, LANES), F32),
                   jax.ShapeDtypeStruct((b, HALO_C, dc), F32)],
        scratch_shapes=[pltpu.VMEM((HALO_C + ts, dc), F32)],
        compiler_params=_params("parallel", "arbitrary"),
        name="odd_mid",
    )(*args)


def _kv_up_body(c_ref, kr_ref, wk_ref, wvt_ref, k_ref, vt_ref):
    c = c_ref[...].astype(BF16)
    k = jnp.dot(c, wk_ref[...], preferred_element_type=F32)
    vt = lax.dot_general(wvt_ref[...], c, (((1,), (1,)), ((), ())), preferred_element_type=F32)
    kr = kr_ref[...].astype(BF16)
    tk = vt_ref.shape[-1]
    for h in range(k.shape[1] // NOPE_DIM):
        k_ref[:, h * HEAD_PAD:h * HEAD_PAD + NOPE_DIM] = k[:, h * NOPE_DIM:(h + 1) * NOPE_DIM].astype(BF16)
        k_ref[:, h * HEAD_PAD + NOPE_DIM:(h + 1) * HEAD_PAD] = kr
        for t in range(vt_ref.shape[2]):
            vt_ref[0, h, t, 0:V_DIM, :] = vt[h * V_DIM:(h + 1) * V_DIM, t * tk:(t + 1) * tk].astype(BF16)
            vt_ref[0, h, t, V_DIM:V_ROWS, :] = jnp.ones((V_ROWS - V_DIM, tk), BF16)


def _kv_up(c, kr, wk, wv_t, b, skv, tk):
    t, r = c.shape
    nkt = skv // tk
    tm = _tile(skv, KV_ROWS_STEP)
    kts = tm // tk
    per = skv // tm
    hpt = 8
    return pl.pallas_call(
        _kv_up_body,
        grid=(t // tm, N_HEADS // hpt),
        in_specs=[pl.BlockSpec((tm, r), lambda i, j: (i, 0)),
                  pl.BlockSpec((tm, LANES), lambda i, j: (i, 0)),
                  pl.BlockSpec((r, hpt * NOPE_DIM), lambda i, j: (0, j)),
                  pl.BlockSpec((hpt * V_DIM, r), lambda i, j: (j, 0))],
        out_specs=[pl.BlockSpec((tm, hpt * HEAD_PAD), lambda i, j: (i, j)),
                   pl.BlockSpec((1, hpt, kts, V_ROWS, tk), lambda i, j: (i // per, j, i % per, 0, 0))],
        out_shape=[jax.ShapeDtypeStruct((t, N_HEADS * HEAD_PAD), BF16),
                   jax.ShapeDtypeStruct((b, N_HEADS, nkt, V_ROWS, tk), BF16)],
        compiler_params=_params("parallel", "arbitrary"),
        name="kv_up",
    )(c, kr, wk, wv_t)


def _colmax(s):
    while s.shape[0] > 8 and s.shape[0] % 16 == 0:
        half = s.shape[0] // 2
        s = jnp.maximum(s[:half], s[half:])
    return jnp.max(s, axis=0, keepdims=True)


def _attention_body(qt_ref, k_ref, vt_ref, o_ref, s_ref, p_ref, acc_ref, *, tq, tk, past, t_valid):
    def k_tile(kt):
        return k_ref[0, kt * tk:(kt + 1) * tk, :]

    def half_step(kt, n_steps, carry, qt, q0, masked):
        m, alpha = carry
        slot, other = kt % 2, 1 - kt % 2
        if kt >= 1:
            pv = jnp.dot(vt_ref[0, 0, kt - 1], p_ref[other], preferred_element_type=F32)
        if kt + 1 < n_steps:
            s_next = jnp.dot(k_tile(kt + 1), qt, preferred_element_type=F32)
        s = s_ref[slot]
        if masked:
            k_pos = kt * tk + lax.broadcasted_iota(jnp.int32, (tk, 1), 0)
            q_pos = q0 + lax.broadcasted_iota(jnp.int32, (1, tq), 1)
            k_chunk = jnp.where(k_pos < t_valid, k_pos >> CHUNK_SHIFT, jnp.iinfo(jnp.int32).max)
            s = jnp.where(k_chunk <= q_pos >> CHUNK_SHIFT, s, MASK_VALUE)
        m_new = jnp.maximum(m, _colmax(s))
        p_ref[slot] = jnp.exp2(s - m_new).astype(BF16)
        if kt == 1:
            acc_ref[...] = pv
        elif kt > 1:
            acc_ref[...] = alpha * acc_ref[...] + pv
        if kt + 1 < n_steps:
            s_ref[other] = s_next
        return m_new, jnp.exp2(m - m_new)

    for qi in range(qt_ref.shape[2]):
        qt = qt_ref[0, 0, qi]
        q0 = past + qi * tq
        n_full = min(((q0 >> CHUNK_SHIFT) + 1) * CHUNK, t_valid) // tk
        n_all = -(-min((((q0 + tq - 1) >> CHUNK_SHIFT) + 1) * CHUNK, t_valid) // tk)
        s_ref[0] = jnp.dot(k_tile(0), qt, preferred_element_type=F32)
        carry = (jnp.full((1, tq), MASK_VALUE, F32), None)
        for kt in range(n_all):
            carry = half_step(kt, n_all, carry, qt=qt, q0=q0, masked=kt >= n_full)
        acc = jnp.dot(vt_ref[0, 0, n_all - 1], p_ref[(n_all - 1) % 2], preferred_element_type=F32)
        if n_all >= 2:
            acc = carry[1] * acc_ref[...] + acc
        o_ref[0, qi * tq:(qi + 1) * tq, :] = jnp.transpose(acc[:V_DIM] / acc[V_DIM:V_DIM + 1]).astype(BF16)


def _attention(qt, k3, vt, past, t_valid):
    b, _, nq, _, tq = qt.shape
    skv = k3.shape[1]
    nkt, tk = vt.shape[2], vt.shape[4]
    return pl.pallas_call(
        functools.partial(_attention_body, tq=tq, tk=tk, past=past, t_valid=t_valid),
        scratch_shapes=[pltpu.VMEM((2, tk, tq), F32), pltpu.VMEM((2, tk, tq), BF16), pltpu.VMEM((V_ROWS, tq), F32)],
        grid=(b, N_HEADS),
        in_specs=[pl.BlockSpec((1, 1, nq, HEAD_PAD, tq), lambda bi, h: (bi, h, 0, 0, 0)),
                  pl.BlockSpec((1, skv, HEAD_PAD), lambda bi, h: (bi, 0, h)),
                  pl.BlockSpec((1, 1, nkt, V_ROWS, tk), lambda bi, h: (bi, h, 0, 0, 0))],
        out_specs=pl.BlockSpec((1, nq * tq, V_DIM), lambda bi, h: (bi, 0, h)),
        out_shape=jax.ShapeDtypeStruct((b, nq * tq, N_HEADS * V_DIM), BF16),
        compiler_params=_params("parallel", "parallel"),
        name="attention",
    )(qt, k3, vt)


def _latent_attention_body(qn_ref, wq_ref, wuk_ref, cs_ref, lat_ref, krp_ref, cnew_ref, krnew_ref, wuv_ref,
                           o_ref, q_ref, acc_ref, *, ss, tk, n_cache, past, scale, rkv):
    qn = qn_ref[0]
    cs = cs_ref[...]
    for h in range(N_HEADS):
        qh = jnp.dot(qn, wq_ref[:, h * HEAD_PAD:(h + 1) * HEAD_PAD], preferred_element_type=F32)
        q_abs = lax.dot_general(qh[:, :NOPE_DIM].astype(BF16), wuk_ref[:, h * NOPE_DIM:(h + 1) * NOPE_DIM],
                                (((1,), (1,)), ((), ())), preferred_element_type=F32)
        q_ref[h * ss:(h + 1) * ss, 0:rkv] = (q_abs * scale).astype(BF16)
        q_ref[h * ss:(h + 1) * ss, rkv:rkv + LANES] = (_rope_fold(qh[:, NOPE_DIM:], cs) * scale).astype(BF16)

    q_lat = q_ref[:, 0:rkv]
    q_rope = q_ref[:, rkv:rkv + ROPE_DIM]
    cols = N_HEADS * ss
    q_chunk = (past + lax.broadcasted_iota(jnp.int32, (1, cols), 1) % ss) >> CHUNK_SHIFT
    acc_ref[...] = jnp.zeros((rkv, cols), F32)
    nt = (((1,), (1,)), ((), ()))

    def update(c, kr, k0, carry):
        m, l = carry
        c = c.astype(BF16)
        s = (lax.dot_general(c, q_lat, nt, preferred_element_type=F32)
             + lax.dot_general(kr.astype(BF16), q_rope, nt, preferred_element_type=F32))
        k_chunk = (k0 + lax.broadcasted_iota(jnp.int32, (c.shape[0], 1), 0)) >> CHUNK_SHIFT
        s = jnp.where(k_chunk <= q_chunk, s, MASK_VALUE)
        m_new = jnp.maximum(m, _colmax(s))
        alpha = jnp.exp2(m - m_new)
        p = jnp.exp2(s - m_new)
        pv = lax.dot_general(c, p.astype(BF16), (((0,), (0,)), ((), ())), preferred_element_type=F32)
        acc_ref[...] = alpha * acc_ref[...] + pv
        return m_new, alpha * l + jnp.sum(p, axis=0, keepdims=True)

    def cache_step(kt, carry):
        rows = pl.ds(pl.multiple_of(kt * tk, tk), tk)
        return update(lat_ref[0, rows, :], krp_ref[0, rows, :], kt * tk, carry)

    carry = (jnp.full((1, cols), MASK_VALUE, F32), jnp.zeros((1, cols), F32))
    carry = lax.fori_loop(0, n_cache, cache_step, carry)
    _, l = update(cnew_ref[0], krnew_ref[0, :, 0:ROPE_DIM], past, carry)
    o_lat = jnp.transpose(acc_ref[...] / l).astype(BF16)
    for h in range(N_HEADS):
        o_ref[0, :, h * V_DIM:(h + 1) * V_DIM] = jnp.dot(
            o_lat[h * ss:(h + 1) * ss], wuv_ref[:, h * V_DIM:(h + 1) * V_DIM],
            preferred_element_type=F32).astype(BF16)


def _latent_attention(qn, wq, wuk, cs_tab, lat_prev, kr_prev, c_new, kr_new, wuv, scale):
    b, ss, rq = qn.shape
    past, rkv = lat_prev.shape[1], lat_prev.shape[2]
    tk = min(TK_ATT, past)
    see_cache = min((((past + ss - 1) >> CHUNK_SHIFT) + 1) * CHUNK, past)
    n_cache = -(-see_cache // tk)
    assert past % tk == 0
    whole = lambda a: pl.BlockSpec(a.shape, lambda bi: (0,) * a.ndim)
    per_b = lambda a: pl.BlockSpec((1,) + a.shape[1:], lambda bi: (bi, 0, 0))
    return pl.pallas_call(
        functools.partial(_latent_attention_body, ss=ss, tk=tk, n_cache=n_cache, past=past, scale=scale, rkv=rkv),
        grid=(b,),
        in_specs=[per_b(qn), whole(wq), whole(wuk), whole(cs_tab),
                  per_b(lat_prev), per_b(kr_prev), per_b(c_new), per_b(kr_new), whole(wuv)],
        out_specs=pl.BlockSpec((1, ss, N_HEADS * V_DIM), lambda bi: (bi, 0, 0)),
        out_shape=jax.ShapeDtypeStruct((b, ss, N_HEADS * V_DIM), BF16),
        scratch_shapes=[pltpu.VMEM((N_HEADS * ss, rkv + LANES), BF16), pltpu.VMEM((rkv, N_HEADS * ss), F32)],
        compiler_params=_params("parallel"),
        name="latent_attention",
    )(qn, wq, wuk, cs_tab, lat_prev, kr_prev, c_new, kr_new, wuv)


def _rope_table(past, s):
    half = ROPE_DIM // 2
    inv = ROPE_THETA ** (-jnp.arange(half, dtype=F32) / half)
    ang = (past + jnp.arange(s, dtype=jnp.int32)).astype(F32)[:, None] * inv[None, :]
    cos, sin = jnp.cos(ang), jnp.sin(ang)
    return jnp.concatenate([cos, cos, sin, sin], axis=-1)


def _rot_cols(w):
    half = ROPE_DIM // 2
    return jnp.concatenate([-w[..., half:], w[..., :half]], axis=-1)


def _pad_rows_front(x, rows):
    return jnp.pad(x, ((0, 0), (rows - x.shape[1], 0), (0, 0)))


def _even_layer(h, b, s, hist, wts, want_v):
    (g_mix, w_in, conv_w, conv_b, lag, lab, lvg, lvb, w_sp, b_sp_t, w_out) = wts
    z = _norm_matmul(h, g_mix, w_in, _tile(w_in.shape[1], TN_EVEN))
    outs = _even_mid(z.reshape(b, s, -1), hist, conv_w, conv_b, lag, lab, lvg, lvb, w_sp, b_sp_t, want_v)
    cat, tail = outs[0], outs[1]
    h = _proj_residual([cat.reshape(b * s, -1)], [w_out], h, TN_EVEN)
    return h, tail[:, HALO_A - (CONV_A - 1):], (outs[2] if want_v else None)


def _odd_layer(h, b, s, hist, lat_prev, kr_prev, wts, cs_tab):
    (g_mix, w_in, conv_w, qg, wq, kvg, w_k, w_v, w_out_c, w_out_a) = wts
    dc = conv_w.shape[1]
    rq, rkv = qg.shape[1], kvg.shape[1]
    z = _norm_matmul(h, g_mix, w_in, w_in.shape[1] // 3)
    scale = math.log2(math.e) / math.sqrt(NOPE_DIM + ROPE_DIM)
    if lat_prev is None:
        assert s % LANES == 0 and TS_ODD == TQ_ATT
        cout, qt, ckv, kr, tail = _odd_mid(z.reshape(b, s, -1), hist, conv_w, qg, kvg, cs_tab, dc, rq, rkv,
                                           wq.T, scale)
        tk = min(TK_ATT, s)
        k, vt = _kv_up(ckv.reshape(b * s, rkv), kr.reshape(b * s, LANES), w_k, w_v.T, b, s, tk)
        attn = _attention(qt, k.reshape(b, s, -1), vt, 0, s)
    else:
        cout, qn, ckv, kr, tail = _odd_mid(z.reshape(b, s, -1), hist, conv_w, qg, kvg, cs_tab, dc, rq, rkv)
        attn = _latent_attention(qn, wq, w_k, cs_tab, lat_prev, kr_prev, ckv, kr, w_v, scale)
    h = _proj_residual([cout.reshape(b * s, dc), attn.reshape(b * s, -1)], [w_out_c, w_out_a], h)
    return h, tail[:, HALO_C - (CONV_C - 1):], ckv, kr[..., :ROPE_DIM]


def kernel(x_prompt, x_sample, state_conv_a, state_conv_c, cache_kv_latent, cache_k_rope, norm_mix, norm_ffn, norm_final, w_in_even, conv_a_w, conv_a_b, ln_a_g, ln_a_b, ln_v_g, ln_v_b, w_spatial, b_spatial, w_out_even, w_in_odd, conv_c_w, q_norm_g, w_uq, kv_norm_g, w_ukv, w_out_odd, w_ffn_up, w_ffn_down):
    bp, sp, d = x_prompt.shape
    bs, ss, _ = x_sample.shape
    depth = norm_mix.shape[0]
    da = conv_a_w.shape[-1]
    dc = conv_c_w.shape[-1]
    rq, rkv = q_norm_g.shape[-1], kv_norm_g.shape[-1]
    past = cache_kv_latent.shape[2]
    assert sp % GMLP_CHUNK == 0 and ss <= GMLP_CHUNK and ss >= HALO_A and past % GMLP_CHUNK == 0

    hp = x_prompt.reshape(bp * sp, d)
    hs = x_sample.reshape(bs * ss, d)
    cs_p = _rope_table(0, sp)
    cs_s = _rope_table(past, ss)
    wu_all, wd_all = _cast_bf16(w_ffn_up), _cast_bf16(w_ffn_down)
    w_in_even_bf, w_out_even_bf, w_out_odd_bf = _cast_bf16(w_in_even), _cast_bf16(w_out_even), _cast_bf16(w_out_odd)

    ca_p, ca_s, gv_s, cc_p, cc_s, lat_p, kr_p, lat_s, kr_s = [], [], [], [], [], [], [], [], []
    for l in range(depth):
        p = l // 2
        if l % 2 == 0:
            row = lambda a: a[p].reshape(1, -1)
            gl_p, gl_s = min(sp, GMLP_CHUNK), min(ss, GMLP_CHUNK)
            common = (norm_mix[l], w_in_even_bf[p],
                      jnp.pad(conv_a_w[p], ((0, CONV_A_ROWS - CONV_A), (0, 0))), row(conv_a_b),
                      row(ln_a_g), row(ln_a_b), row(ln_v_g), row(ln_v_b))
            w_out = w_out_even_bf[p]
            wts_p = common + (w_spatial[p][:, :gl_p, :gl_p], b_spatial[p][:, :gl_p].T, w_out)
            wts_s = common + (w_spatial[p][:, :gl_s, :gl_s], b_spatial[p][:, :gl_s].T, w_out)
            hp, tail_p, _ = _even_layer(hp, bp, sp, jnp.zeros((bp, HALO_A, da), F32), wts_p, False)
            hs, tail_s, v_s = _even_layer(hs, bs, ss, _pad_rows_front(state_conv_a[p], HALO_A), wts_s, True)
            ca_p.append(tail_p)
            ca_s.append(tail_s)
            gv_s.append(v_s)
        else:
            w_in = w_in_odd[p]
            n_main = 3 * dc + rq + rkv
            w_in = jnp.concatenate([w_in, _rot_cols(w_in[:, n_main:])], axis=1).astype(BF16)
            wq = w_uq[p].reshape(rq, N_HEADS, NOPE_DIM + ROPE_DIM)
            wq = jnp.concatenate([wq, _rot_cols(wq[..., NOPE_DIM:])], axis=-1).reshape(rq, N_HEADS * HEAD_PAD)
            wq = wq.astype(BF16)
            wkv = w_ukv[p].reshape(rkv, N_HEADS, NOPE_DIM + V_DIM)
            wk = wkv[..., :NOPE_DIM].reshape(rkv, N_HEADS * NOPE_DIM).astype(BF16)
            wv = wkv[..., NOPE_DIM:].reshape(rkv, N_HEADS * V_DIM).astype(BF16)
            w_out = w_out_odd_bf[p]
            wts = (norm_mix[l], w_in, conv_c_w[p], q_norm_g[p].reshape(1, rq), wq,
                   kv_norm_g[p].reshape(1, rkv), wk, wv, w_out[:dc], w_out[dc:])
            hp, tail_p, c_p, r_p = _odd_layer(hp, bp, sp, jnp.zeros((bp, HALO_C, dc), F32), None, None,
                                              wts, cs_p)
            hs, tail_s, c_s, r_s = _odd_layer(hs, bs, ss, _pad_rows_front(state_conv_c[p], HALO_C),
                                              cache_kv_latent[p], cache_k_rope[p], wts, cs_s)
            cc_p.append(tail_p)
            cc_s.append(tail_s)
            lat_p.append(c_p)
            kr_p.append(r_p)
            lat_s.append(c_s)
            kr_s.append(r_s)
        last = l == depth - 1
        hp = _ffn(hp, norm_ffn[l], wu_all, wd_all, l, norm_final, last)
        hs = _ffn(hs, norm_ffn[l], wu_all, wd_all, l, norm_final, last)

    return (hp.reshape(bp, sp, d), hs.reshape(bs, ss, d),
            jnp.stack(ca_p), jnp.stack(ca_s), jnp.stack(gv_s), jnp.stack(cc_p), jnp.stack(cc_s),
            jnp.stack(lat_p), jnp.stack(kr_p), jnp.stack(lat_s), jnp.stack(kr_s))
```
